```python
import jax, jax.numpy as jnp
from jax import lax
import numpy as np

D_MODEL = 1024
BATCH = 8
SEQ = 8192
DEPTH = 2
DEC_BATCH = 32
DEC_SEQ = 2048
PAST_LEN = 128

LRU_WIDTH = 512
LRU_BLOCKS = 8
LRU_BLOCK_W = LRU_WIDTH // LRU_BLOCKS
CONV_W = 4
CONV_LEFT = 2
LRU_C = 8.0
HG_HEADS = 4
HG_DK = 128
HG_DV = 128
HG_K = HG_HEADS * HG_DK
HG_V = HG_HEADS * HG_DV
HG_CHUNK = 64
MIX_WIDTH = LRU_WIDTH + HG_V
IN_SPLITS = (LRU_WIDTH, 2 * LRU_WIDTH, 2 * LRU_WIDTH + HG_K, 2 * LRU_WIDTH + HG_K + HG_V,
             2 * LRU_WIDTH + 2 * HG_K + HG_V, 2 * LRU_WIDTH + 3 * HG_K + HG_V)
IN_COLS = 2 * LRU_WIDTH + 3 * HG_K + 2 * HG_V
N_EXPERTS = 16
N_GROUPS = 4
EXPERTS_PER_GROUP = N_EXPERTS // N_GROUPS
TOP_K = 2
D_FF = 512
ALPHA = (2 * DEPTH) ** 0.25
BETA = (8 * DEPTH) ** -0.25
LN_EPS = 1e-5
RMS_EPS = 1e-6

kernel_name = 'hybrid_rglru_hgrn2_moe_encoder'

F32 = jnp.float32


def layer_norm(x, g, b):
    xf = x.astype(F32)
    mu = jnp.mean(xf, axis=-1, keepdims=True)
    var = jnp.mean(jnp.square(xf - mu), axis=-1, keepdims=True)
    return ((xf - mu) * lax.rsqrt(var + LN_EPS) * g + b).astype(x.dtype)


def centred_dwconv(x, w, b):
    n = x.shape[1]
    xp = jnp.pad(x, ((0, 0), (CONV_LEFT, CONV_W - 1 - CONV_LEFT), (0, 0)))
    y = xp[:, 0:n] * w[0]
    for j in range(1, CONV_W):
        y = y + xp[:, j:j + n] * w[j]
    return y + b


def _lin_combine(left, right):
    a1, b1 = left
    a2, b2 = right
    return a1 * a2, a2 * b1 + b2


def rg_lru_scan(xc, wa, ba, wx, bx, lam, reverse):
    bsz, n, _ = xc.shape
    xh = xc.reshape(bsz, n, LRU_BLOCKS, LRU_BLOCK_W)
    r = jax.nn.sigmoid(jnp.einsum('bnhi,hij->bnhj', xh, wa.astype(F32)).reshape(bsz, n, LRU_WIDTH) + ba.astype(F32))
    ig = jax.nn.sigmoid(jnp.einsum('bnhi,hij->bnhj', xh, wx.astype(F32)).reshape(bsz, n, LRU_WIDTH) + bx.astype(F32))
    log_a = LRU_C * r * jax.nn.log_sigmoid(lam.astype(F32))
    a = jnp.exp(log_a)
    u = jnp.sqrt(-jnp.expm1(2.0 * log_a)) * (ig * xc)
    _, h = lax.associative_scan(_lin_combine, (a, u), axis=1, reverse=reverse)
    return h


def gated_state_scan(q, k, v, log_f, reverse):
    if reverse:
        q, k, v, log_f = (jnp.flip(t, axis=1) for t in (q, k, v, log_f))
    bsz, n, h, dk = q.shape
    dv = v.shape[-1]
    nc = n // HG_CHUNK

    def to_chunks(t):
        return t.reshape(bsz, nc, HG_CHUNK, h, t.shape[-1]).transpose(1, 0, 3, 2, 4)

    mask = jnp.tril(jnp.ones((HG_CHUNK, HG_CHUNK), dtype=bool))[:, :, None]

    def step(state, inp):
        qc, kc, vc, gc = inp
        cum = jnp.cumsum(gc, axis=2)
        diff = cum[:, :, :, None, :] - cum[:, :, None, :, :]
        decay = jnp.where(mask, jnp.exp(jnp.where(mask, diff, 0.0)), 0.0)
        scores = jnp.einsum('bhtk,bhsk,bhtsk->bhts', qc, kc, decay)
        o = jnp.einsum('bhts,bhsv->bhtv', scores, vc) + jnp.einsum('bhtk,bhkv->bhtv', qc * jnp.exp(cum), state)
        last = cum[:, :, -1:, :]
        state = jnp.exp(last[:, :, 0, :])[..., None] * state + jnp.einsum('bhsk,bhsv->bhkv', kc * jnp.exp(last - cum), vc)
        return state, o

    s0 = jnp.zeros((bsz, h, dk, dv), F32)
    _, o = lax.scan(step, s0, (to_chunks(q), to_chunks(k), to_chunks(v), to_chunks(log_f)))
    o = o.transpose(1, 0, 3, 2, 4).reshape(bsz, n, h, dv)
    if reverse:
        o = jnp.flip(o, axis=1)
    return o


def hgrn2_group(u_q, u_i, u_f_fwd, u_f_bwd, u_g, lb, norm_g):
    bsz, n, _ = u_q.shape

    def heads(t):
        return t.astype(F32).reshape(bsz, n, HG_HEADS, -1)

    q = jax.nn.silu(heads(u_q)) * (HG_DK ** -0.5)
    v = heads(u_i)
    lb_h = lb.reshape(2, HG_HEADS, HG_DK)
    f_fwd = lb_h[0] + (1.0 - lb_h[0]) * jax.nn.sigmoid(heads(u_f_fwd))
    f_bwd = lb_h[1] + (1.0 - lb_h[1]) * jax.nn.sigmoid(heads(u_f_bwd))
    o = (gated_state_scan(q, 1.0 - f_fwd, v, jnp.log(f_fwd), False)
         + gated_state_scan(q, 1.0 - f_bwd, v, jnp.log(f_bwd), True))
    o = o * lax.rsqrt(jnp.mean(o * o, axis=-1, keepdims=True) + RMS_EPS)
    return o.reshape(bsz, n, HG_V) * norm_g.astype(F32) * jax.nn.silu(u_g.astype(F32))


def grouped_moe(x, w_router, b_router, w_gate, w_up, w_down):
    logits = jnp.einsum('bnd,de->bne', x, w_router).astype(F32)
    probs = jax.nn.softmax(logits, axis=-1)
    sel = probs + b_router.astype(F32)
    grp = sel.reshape(sel.shape[:-1] + (N_GROUPS, EXPERTS_PER_GROUP))
    grp_score = jnp.sum(lax.top_k(grp, TOP_K)[0], axis=-1)
    best = jnp.argmax(grp_score, axis=-1)
    in_group = (jnp.arange(N_EXPERTS) // EXPERTS_PER_GROUP) == best[..., None]
    _, idx = lax.top_k(jnp.where(in_group, sel, -jnp.inf), TOP_K)
    w = jnp.take_along_axis(probs, idx, axis=-1)
    w = w / jnp.sum(w, axis=-1, keepdims=True)
    gates = jnp.einsum('bnk,bnke->bne', w, jax.nn.one_hot(idx, N_EXPERTS, dtype=F32))
    y = jnp.zeros(x.shape, F32)
    for e in range(N_EXPERTS):
        hdn = jax.nn.silu(x @ w_gate[e]) * (x @ w_up[e])
        y = y + gates[..., e:e + 1] * (hdn @ w_down[e])
    return y.astype(x.dtype)


def trunk_layer(x, w_in, conv_w, conv_b, lru_wa, lru_ba, lru_wx, lru_bx, lru_lambda, lb, hgrn_norm_g,
                w_out, ln1_g, ln1_b, w_router, b_router, w_gate, w_up, w_down, ln2_g, ln2_b):
    u = jnp.einsum('bnd,dc->bnc', x, w_in)
    u_x, u_gate, u_q, u_i, u_ff, u_fb, u_g = jnp.split(u, IN_SPLITS, axis=-1)
    xc = centred_dwconv(u_x, conv_w, conv_b).astype(F32)
    h = (rg_lru_scan(xc, lru_wa[0], lru_ba[0], lru_wx[0], lru_bx[0], lru_lambda[0], False)
         + rg_lru_scan(xc, lru_wa[1], lru_ba[1], lru_wx[1], lru_bx[1], lru_lambda[1], True))
    lru_out = h * jax.nn.gelu(u_gate.astype(F32))
    hg_out = hgrn2_group(u_q, u_i, u_ff, u_fb, u_g, lb, hgrn_norm_g)
    mix = jnp.concatenate([lru_out, hg_out], axis=-1).astype(x.dtype) @ w_out
    x = layer_norm(ALPHA * x + mix, ln1_g, ln1_b)
    x = layer_norm(ALPHA * x + grouped_moe(x, w_router, b_router, w_gate, w_up, w_down), ln2_g, ln2_b)
    return x


def run_trunk(x, ln_in_g, ln_in_b, w_in, conv_w, conv_b, lru_wa, lru_ba, lru_wx, lru_bx, lru_lambda,
              lb_all, hgrn_norm_g, w_out, ln1_g, ln1_b, w_router, b_router, w_gate, w_up, w_down, ln2_g, ln2_b):
    x = layer_norm(x, ln_in_g, ln_in_b)
    for l in range(DEPTH):
        x = trunk_layer(x, w_in[l], conv_w[l], conv_b[l], lru_wa[l], lru_ba[l], lru_wx[l], lru_bx[l],
                        lru_lambda[l], lb_all[:, l], hgrn_norm_g[l], w_out[l], ln1_g[l], ln1_b[l],
                        w_router, b_router, w_gate[l], w_up[l], w_down[l], ln2_g[l], ln2_b[l])
    return x


def setup_inputs(seed: int = 0) -> dict:
    key = jax.random.key(seed)
    ks = jax.random.split(key, 26)
    nrm = jax.random.normal
    lam_u = jax.random.uniform(ks[10], (DEPTH, 2, LRU_WIDTH), F32, minval=0.9, maxval=0.999)
    a0 = lam_u ** (1.0 / LRU_C)
    return {
        'x_prompt': nrm(ks[0], (BATCH, SEQ, D_MODEL), F32),
        'x_sample': nrm(ks[1], (DEC_BATCH, DEC_SEQ, D_MODEL), F32),
        'ln_in_g': 1.0 + 0.02 * nrm(ks[2], (D_MODEL,), F32),
        'ln_in_b': 0.02 * nrm(ks[3], (D_MODEL,), F32),
        'w_in': nrm(ks[4], (DEPTH, D_MODEL, IN_COLS), F32) * D_MODEL ** -0.5,
        'conv_w': nrm(ks[5], (DEPTH, CONV_W, LRU_WIDTH), F32) * CONV_W ** -0.5,
        'conv_b': 0.01 * nrm(ks[6], (DEPTH, LRU_WIDTH), F32),
        'lru_wa': nrm(ks[7], (DEPTH, 2, LRU_BLOCKS, LRU_BLOCK_W, LRU_BLOCK_W), F32) * LRU_BLOCK_W ** -0.5,
        'lru_ba': 0.01 * nrm(ks[8], (DEPTH, 2, LRU_WIDTH), F32),
        'lru_wx': nrm(ks[9], (DEPTH, 2, LRU_BLOCKS, LRU_BLOCK_W, LRU_BLOCK_W), F32) * LRU_BLOCK_W ** -0.5,
        'lru_bx': 0.01 * nrm(ks[11], (DEPTH, 2, LRU_WIDTH), F32),
        'lru_lambda': jnp.log(a0) - jnp.log1p(-a0),
        'hgrn_lb': nrm(ks[12], (2, DEPTH, HG_K), F32),
        'hgrn_norm_g': 1.0 + 0.02 * nrm(ks[13], (DEPTH, HG_V), F32),
        'w_out': nrm(ks[14], (DEPTH, MIX_WIDTH, D_MODEL), F32) * (MIX_WIDTH ** -0.5 * BETA),
        'ln1_g': 1.0 + 0.02 * nrm(ks[15], (DEPTH, D_MODEL), F32),
        'ln1_b': 0.02 * nrm(ks[16], (DEPTH, D_MODEL), F32),
        'w_router': nrm(ks[17], (D_MODEL, N_EXPERTS), F32) * D_MODEL ** -0.5,
        'b_router': 0.01 * nrm(ks[18], (N_EXPERTS,), F32),
        'w_gate': nrm(ks[19], (DEPTH, N_EXPERTS, D_MODEL, D_FF), F32) * D_MODEL ** -0.5,
        'w_up': nrm(ks[20], (DEPTH, N_EXPERTS, D_MODEL, D_FF), F32) * D_MODEL ** -0.5,
        'w_down': nrm(ks[21], (DEPTH, N_EXPERTS, D_FF, D_MODEL), F32) * (D_FF ** -0.5 * BETA),
        'ln2_g': 1.0 + 0.02 * nrm(ks[22], (DEPTH, D_MODEL), F32),
        'ln2_b': 0.02 * nrm(ks[23], (DEPTH, D_MODEL), F32),
    }


def reference(x_prompt, x_sample, ln_in_g, ln_in_b, w_in, conv_w, conv_b, lru_wa, lru_ba, lru_wx, lru_bx,
              lru_lambda, hgrn_lb, hgrn_norm_g, w_out, ln1_g, ln1_b, w_router, b_router, w_gate, w_up, w_down,
              ln2_g, ln2_b):
    p = jax.nn.softmax(hgrn_lb.astype(F32), axis=1)
    lb_all = jnp.cumsum(p, axis=1) - p[:, :1]
    y_prompt = run_trunk(x_prompt, ln_in_g, ln_in_b, w_in, conv_w, conv_b, lru_wa, lru_ba, lru_wx, lru_bx,
                         lru_lambda, lb_all, hgrn_norm_g, w_out, ln1_g, ln1_b, w_router, b_router,
                         w_gate, w_up, w_down, ln2_g, ln2_b)
    y_sample = run_trunk(x_sample, ln_in_g, ln_in_b, w_in, conv_w, conv_b, lru_wa, lru_ba, lru_wx, lru_bx,
                         lru_lambda, lb_all, hgrn_norm_g, w_out, ln1_g, ln1_b, w_router, b_router,
                         w_gate, w_up, w_down, ln2_g, ln2_b)
    return (y_prompt, y_sample)
```

```python
import functools

import jax
import jax.numpy as jnp
from jax import lax
from jax.experimental import pallas as pl
from jax.experimental.pallas import tpu as pltpu

F32 = jnp.float32
BF16 = jnp.bfloat16

LRU_WIDTH = 512
LRU_BLOCKS = 8
CONV_W = 4
CONV_LEFT = 2
LRU_C = 8.0
HG_HEADS = 4
HG_DK = 128
HG_DV = 128
HG_W = HG_HEADS * HG_DK
N_EXPERTS = 16
N_GROUPS = 4
EXPERTS_PER_GROUP = N_EXPERTS // N_GROUPS
LN_EPS = 1e-5
RMS_EPS = 1e-6
COL_LRU_X, COL_LRU_GATE, COL_Q, COL_I, COL_F_FWD, COL_F_BWD, COL_G = range(7)

LANES = 128
SUBLANES = 8
VMEM_LIMIT_BYTES = 56 * 1024 * 1024

TOKEN_TILE = 512
LRU_TIME_TILE = 512
HG_TIME_TILE = 512
HG_CHUNK = 64
MOE_TOKEN_TILE = 1024
HG_EXP_CLAMP = 80.0


def _params(n_axes):
    return pltpu.CompilerParams(dimension_semantics=("arbitrary",) * n_axes,
                                vmem_limit_bytes=VMEM_LIMIT_BYTES)


def _layer_norm(x, g, b):
    mu = jnp.mean(x, axis=-1, keepdims=True)
    xc = x - mu
    var = jnp.mean(xc * xc, axis=-1, keepdims=True)
    return xc * lax.rsqrt(var + LN_EPS) * g + b


def _sigmoid(x):
    return 1.0 / (1.0 + jnp.exp(-x))


def _silu(x):
    return x * _sigmoid(x)


def _in_proj_ln_kernel(x_ref, g_ref, b_ref, w_ref, u_ref, xn_ref):
    xn = _layer_norm(x_ref[...], g_ref[...], b_ref[...])
    xn_ref[...] = xn
    u_ref[...] = jnp.dot(xn.astype(BF16), w_ref[...], preferred_element_type=F32)


def _in_proj_kernel(x_ref, w_ref, u_ref):
    u_ref[...] = jnp.dot(x_ref[...].astype(BF16), w_ref[...], preferred_element_type=F32)


def _in_proj(x2d, w_in_bf, ln=None):
    t, d = x2d.shape
    cols = w_in_bf.shape[1]
    tm = min(TOKEN_TILE, t)
    grid = (t // tm,)
    x_spec = pl.BlockSpec((tm, d), lambda i: (i, 0))
    w_spec = pl.BlockSpec((d, cols), lambda i: (0, 0))
    u_spec = pl.BlockSpec((tm, cols), lambda i: (i, 0))
    vec_spec = pl.BlockSpec((1, d), lambda i: (0, 0))
    if ln is None:
        u = pl.pallas_call(
            _in_proj_kernel, grid=grid, in_specs=[x_spec, w_spec], out_specs=u_spec,
            out_shape=jax.ShapeDtypeStruct((t, cols), F32), compiler_params=_params(1),
            name="in_proj")(x2d, w_in_bf)
        return u, x2d
    g, b = ln
    u, xn = pl.pallas_call(
        _in_proj_ln_kernel, grid=grid, in_specs=[x_spec, vec_spec, vec_spec, w_spec],
        out_specs=[u_spec, x_spec],
        out_shape=[jax.ShapeDtypeStruct((t, cols), F32), jax.ShapeDtypeStruct((t, d), F32)],
        compiler_params=_params(1), name="in_proj_ln")(x2d, g.reshape(1, d), b.reshape(1, d), w_in_bf)
    return u, xn


def _lru_kernel(cur_ref, prev_ref, next_ref, cw_ref, cb_ref, wg_ref, bg_ref, lam_ref, h_ref,
                a_scr, u_scr, c_scr, *, reverse, tc):
    t = pl.program_id(1)
    nt = pl.num_programs(1)
    tt = (nt - 1 - t) if reverse else t
    w = LRU_WIDTH

    @pl.when(t == 0)
    def _():
        c_scr[...] = jnp.zeros_like(c_scr)

    cur = cur_ref[...]
    prev = jnp.where(tt > 0, prev_ref[...], 0.0)
    nxt = jnp.where(tt < nt - 1, next_ref[...], 0.0)
    row8 = lax.broadcasted_iota(jnp.int32, (SUBLANES, w), 0)

    def shift_down(k):
        s = pltpu.roll(cur, k, axis=0)
        head = jnp.where(row8 < k, pltpu.roll(prev, k, axis=0), s[0:SUBLANES])
        return jnp.concatenate([head, s[SUBLANES:]], axis=0)

    s_up = pltpu.roll(cur, tc - 1, axis=0)
    tail = jnp.where(row8 == SUBLANES - 1, pltpu.roll(nxt, SUBLANES - 1, axis=0), s_up[tc - SUBLANES:])
    x_p1 = jnp.concatenate([s_up[:tc - SUBLANES], tail], axis=0)

    cw = cw_ref[...]
    xc = shift_down(2) * cw[0:1] + shift_down(1) * cw[1:2]
    xc = xc + cur * cw[2:3]
    xc = xc + x_p1 * cw[3:4]
    xc = xc + cb_ref[...]

    z = jnp.dot(xc.astype(BF16), wg_ref[...], preferred_element_type=F32) + bg_ref[...]
    r = _sigmoid(z[:, :w])
    ig = _sigmoid(z[:, w:])
    lam = lam_ref[...]
    e = jnp.exp(-jnp.abs(lam))
    e1 = 1.0 + e
    d = e1 - 1.0
    log1p_e = jnp.where(d == 0.0, e, jnp.log(e1) * (e / jnp.where(d == 0.0, 1.0, d)))
    log_a = (LRU_C * (jnp.minimum(lam, 0.0) - log1p_e)) * r
    a = jnp.exp(log_a)
    th = jnp.tanh(log_a)
    u = jnp.sqrt(-2.0 * th / (1.0 - th)) * (ig * xc)

    rm = lax.broadcasted_iota(jnp.int32, (tc, w), 0) % SUBLANES
    for k in (1, 2, 4):
        if reverse:
            keep = rm < SUBLANES - k
            a_s = jnp.where(keep, pltpu.roll(a, tc - k, axis=0), 1.0)
            u_s = jnp.where(keep, pltpu.roll(u, tc - k, axis=0), 0.0)
        else:
            keep = rm >= k
            a_s = jnp.where(keep, pltpu.roll(a, k, axis=0), 1.0)
            u_s = jnp.where(keep, pltpu.roll(u, k, axis=0), 0.0)
        u = u + a * u_s
        a = a * a_s
    a_scr[...] = a
    u_scr[...] = u

    ng = tc // SUBLANES

    def body(i, c):
        g = (ng - 1 - i) if reverse else i
        r0 = pl.multiple_of(g * SUBLANES, SUBLANES)
        hh = u_scr[pl.ds(r0, SUBLANES), :] + a_scr[pl.ds(r0, SUBLANES), :] * c
        h_ref[pl.ds(r0, SUBLANES), :] = hh
        return hh[0:1, :] if reverse else hh[SUBLANES - 1:SUBLANES, :]

    c_scr[...] = lax.fori_loop(0, ng, body, c_scr[...], unroll=8)


def _lru_scan(u3, conv_w, conv_b, w_gates_bf, b_gates, lam, reverse):
    bsz, n, _ = u3.shape
    w = LRU_WIDTH
    tc = min(LRU_TIME_TILE, n)
    nt = n // tc
    nb8 = n // SUBLANES
    per = tc // SUBLANES

    def tmap(t):
        return (nt - 1 - t) if reverse else t

    in_specs = [
        pl.BlockSpec((None, tc, w), lambda b, t: (b, tmap(t), COL_LRU_X)),
        pl.BlockSpec((None, SUBLANES, w), lambda b, t: (b, jnp.maximum(tmap(t) * per - 1, 0), COL_LRU_X)),
        pl.BlockSpec((None, SUBLANES, w), lambda b, t: (b, jnp.minimum((tmap(t) + 1) * per, nb8 - 1), COL_LRU_X)),
        pl.BlockSpec((CONV_W, w), lambda b, t: (0, 0)),
        pl.BlockSpec((1, w), lambda b, t: (0, 0)),
        pl.BlockSpec((w, 2 * w), lambda b, t: (0, 0)),
        pl.BlockSpec((1, 2 * w), lambda b, t: (0, 0)),
        pl.BlockSpec((1, w), lambda b, t: (0, 0)),
    ]
    out_spec = pl.BlockSpec((None, tc, w), lambda b, t: (b, tmap(t), 0))
    return pl.pallas_call(
        functools.partial(_lru_kernel, reverse=reverse, tc=tc),
        grid=(bsz, nt), in_specs=in_specs, out_specs=out_spec,
        out_shape=jax.ShapeDtypeStruct((bsz, n, w), F32),
        scratch_shapes=[pltpu.VMEM((tc, w), F32), pltpu.VMEM((tc, w), F32), pltpu.VMEM((1, w), F32)],
        compiler_params=_params(2), name="lru_bwd" if reverse else "lru_fwd",
    )(u3, u3, u3, conv_w, conv_b.reshape(1, w), w_gates_bf, b_gates.reshape(1, 2 * w), lam.reshape(1, w))


def _hgrn_kernel(q_ref, v_ref, f_ref, lb_ref, o_ref, st_scr, *, reverse, layer, tb, c):
    t = pl.program_id(2)

    @pl.when(t == 0)
    def _():
        st_scr[...] = jnp.zeros_like(st_scr)

    z = lb_ref[...]
    e = jnp.exp(z - jnp.max(z, axis=0, keepdims=True))
    p = e / jnp.sum(e, axis=0, keepdims=True)
    lb = jnp.sum(p[0:layer + 1], axis=0, keepdims=True) - p[0:1]

    ri = lax.broadcasted_iota(jnp.int32, (c, c), 0)
    ci = lax.broadcasted_iota(jnp.int32, (c, c), 1)
    valid = (ci >= ri) if reverse else (ci <= ri)
    tri = jnp.where(valid, 1.0, 0.0).astype(BF16)
    half = c // 2
    scale = HG_DK ** -0.5
    nchunks = tb // c
    nt_dims = (((1,), (1,)), ((), ()))
    tn_dims = (((0,), (0,)), ((), ()))

    st = st_scr[...]
    for j in (range(nchunks - 1, -1, -1) if reverse else range(nchunks)):
        sl = slice(j * c, (j + 1) * c)
        uq = q_ref[sl, :]
        v = v_ref[sl, :]
        q = _silu(uq) * scale
        f = lb + (1.0 - lb) * _sigmoid(f_ref[sl, :])
        k = 1.0 - f
        g = jnp.log(f)
        g_hi = g.astype(BF16)
        r1 = g - g_hi.astype(F32)
        g_mid = r1.astype(BF16)
        g_lo = (r1 - g_mid.astype(F32)).astype(BF16)
        cum3 = jnp.dot(tri, jnp.concatenate([g_hi, g_mid, g_lo], axis=1), preferred_element_type=F32)
        cum = (cum3[:, 0:HG_DK] + cum3[:, HG_DK:2 * HG_DK]) + cum3[:, 2 * HG_DK:3 * HG_DK]
        if reverse:
            mid = cum[half:half + 1]
            tot = cum[0:1]
        else:
            mid = cum[half - 1:half]
            tot = cum[c - 1:c]
        qt = q * jnp.exp(jnp.minimum(cum - mid, HG_EXP_CLAMP))
        kt = k * jnp.exp(jnp.minimum(mid - cum, HG_EXP_CLAMP))
        scores = lax.dot_general(qt.astype(BF16), kt.astype(BF16), nt_dims, preferred_element_type=F32)
        scores = jnp.where(valid, scores, 0.0)
        vb = v.astype(BF16)
        o = jnp.dot(scores.astype(BF16), vb, preferred_element_type=F32)
        qs = (q * jnp.exp(cum)).astype(BF16)
        o = o + lax.dot_general(qs, st.astype(BF16), nt_dims, preferred_element_type=F32)
        o_ref[sl, :] = o
        ks = (k * jnp.exp(tot - cum)).astype(BF16)
        st = st * jnp.exp(tot) + lax.dot_general(vb, ks, tn_dims, preferred_element_type=F32)
    st_scr[...] = st


def _hgrn_scan(u3, lb_raw, layer, reverse):
    bsz, n, _ = u3.shape
    depth = lb_raw.shape[0]
    tb = min(HG_TIME_TILE, n)
    nt = n // tb
    c = min(HG_CHUNK, tb)
    per = LRU_WIDTH // HG_DK
    f_col = COL_F_BWD if reverse else COL_F_FWD

    def tmap(t):
        return (nt - 1 - t) if reverse else t

    in_specs = [
        pl.BlockSpec((None, tb, HG_DK), lambda b, h, t: (b, tmap(t), COL_Q * per + h)),
        pl.BlockSpec((None, tb, HG_DV), lambda b, h, t: (b, tmap(t), COL_I * per + h)),
        pl.BlockSpec((None, tb, HG_DK), lambda b, h, t: (b, tmap(t), f_col * per + h)),
        pl.BlockSpec((depth, HG_DK), lambda b, h, t: (0, h)),
    ]
    out_spec = pl.BlockSpec((None, tb, HG_DV), lambda b, h, t: (b, tmap(t), h))
    return pl.pallas_call(
        functools.partial(_hgrn_kernel, reverse=reverse, layer=layer, tb=tb, c=c),
        grid=(bsz, HG_HEADS, nt), in_specs=in_specs, out_specs=out_spec,
        out_shape=jax.ShapeDtypeStruct((bsz, n, HG_W), F32),
        scratch_shapes=[pltpu.VMEM((HG_DV, HG_DK), F32)],
        compiler_params=_params(3), name="hgrn_bwd" if reverse else "hgrn_fwd",
    )(u3, u3, u3, lb_raw)


def _route(logits_t, b_router):
    mx = jnp.max(logits_t, axis=0, keepdims=True)
    ex = jnp.exp(logits_t - mx)
    probs = ex / jnp.sum(ex, axis=0, keepdims=True)
    sel = probs + b_router
    rows = [sel[i:i + 1] for i in range(N_EXPERTS)]
    prow = [probs[i:i + 1] for i in range(N_EXPERTS)]
    gscore = []
    for gi in range(N_GROUPS):
        m = rows[gi * EXPERTS_PER_GROUP:(gi + 1) * EXPERTS_PER_GROUP]
        best = None
        for a in range(EXPERTS_PER_GROUP):
            for b in range(a + 1, EXPERTS_PER_GROUP):
                s = m[a] + m[b]
                best = s if best is None else jnp.maximum(best, s)
        gscore.append(best)
    gbest = jnp.zeros_like(gscore[0], dtype=jnp.int32)
    gmax = gscore[0]
    for gi in range(1, N_GROUPS):
        better = gscore[gi] > gmax
        gbest = jnp.where(better, gi, gbest)
        gmax = jnp.where(better, gscore[gi], gmax)
    ms, mp = [], []
    for j in range(EXPERTS_PER_GROUP):
        s = rows[j]
        pr = prow[j]
        for gi in range(1, N_GROUPS):
            pick = gbest == gi
            s = jnp.where(pick, rows[gi * EXPERTS_PER_GROUP + j], s)
            pr = jnp.where(pick, prow[gi * EXPERTS_PER_GROUP + j], pr)
        ms.append(s)
        mp.append(pr)
    i1 = jnp.zeros_like(gbest)
    v1 = ms[0]
    for j in range(1, EXPERTS_PER_GROUP):
        better = ms[j] > v1
        i1 = jnp.where(better, j, i1)
        v1 = jnp.where(better, ms[j], v1)
    i2 = jnp.full_like(gbest, -1)
    v2 = jnp.full_like(v1, -jnp.inf)
    for j in range(EXPERTS_PER_GROUP):
        better = (i1 != j) & ((ms[j] > v2) | (i2 < 0))
        i2 = jnp.where(better, j, i2)
        v2 = jnp.where(better, ms[j], v2)
    p1 = mp[0]
    p2 = mp[0]
    for j in range(1, EXPERTS_PER_GROUP):
        p1 = jnp.where(i1 == j, mp[j], p1)
        p2 = jnp.where(i2 == j, mp[j], p2)
    den = p1 + p2
    w1 = p1 / den
    w2 = p2 / den
    e1 = gbest * EXPERTS_PER_GROUP + i1
    e2 = gbest * EXPERTS_PER_GROUP + i2
    eid = lax.broadcasted_iota(jnp.int32, logits_t.shape, 0)
    return jnp.where(eid == e1, w1, 0.0) + jnp.where(eid == e2, w2, 0.0)


def _mix_kernel(x_ref, hf_ref, hb_ref, ug_ref, of_ref, ob_ref, uhg_ref, ng_ref, wo_ref, g1_ref, b1_ref,
                wr_ref, br_ref, x1_ref, gates_ref, *, alpha):
    tm = x_ref.shape[0]
    lru = (hf_ref[...] + hb_ref[...]) * jax.nn.gelu(ug_ref[...], approximate=True)
    o = of_ref[...] + ob_ref[...]
    heads = []
    for h in range(HG_HEADS):
        oh = o[:, h * HG_DV:(h + 1) * HG_DV]
        heads.append(oh * lax.rsqrt(jnp.mean(oh * oh, axis=-1, keepdims=True) + RMS_EPS))
    hg = jnp.concatenate(heads, axis=1) * ng_ref[...] * _silu(uhg_ref[...])
    mix_in = jnp.concatenate([lru, hg], axis=1).astype(BF16)
    mix = jnp.dot(mix_in, wo_ref[...], preferred_element_type=F32)
    x1 = _layer_norm(alpha * x_ref[...] + mix, g1_ref[...], b1_ref[...])
    x1_ref[...] = x1
    logits_t = lax.dot_general(wr_ref[...], x1, (((1,), (1,)), ((), ())),
                               precision=lax.Precision.HIGHEST, preferred_element_type=F32)
    gates_t = _route(logits_t[0:N_EXPERTS], br_ref[...])
    pad = jnp.zeros((LANES - N_EXPERTS, tm), F32)
    gates_ref[...] = jnp.concatenate([gates_t, pad], axis=0).T


def _mix(x2d, hf, hb, u2d, of, ob, norm_g, w_out_bf, ln_g, ln_b, w_router_t, b_router, alpha):
    t, d = x2d.shape
    w = LRU_WIDTH
    tm = min(TOKEN_TILE, t)
    row = lambda i: (i, 0)
    const = lambda i: (0, 0)
    in_specs = [
        pl.BlockSpec((tm, d), row),
        pl.BlockSpec((tm, w), row),
        pl.BlockSpec((tm, w), row),
        pl.BlockSpec((tm, w), lambda i: (i, COL_LRU_GATE)),
        pl.BlockSpec((tm, w), row),
        pl.BlockSpec((tm, w), row),
        pl.BlockSpec((tm, w), lambda i: (i, COL_G)),
        pl.BlockSpec((1, w), const),
        pl.BlockSpec((2 * w, d), const),
        pl.BlockSpec((1, d), const),
        pl.BlockSpec((1, d), const),
        pl.BlockSpec((LANES, d), const),
        pl.BlockSpec((N_EXPERTS, 1), const),
    ]
    out_specs = [pl.BlockSpec((tm, d), row), pl.BlockSpec((tm, LANES), row)]
    return pl.pallas_call(
        functools.partial(_mix_kernel, alpha=alpha),
        grid=(t // tm,), in_specs=in_specs, out_specs=out_specs,
        out_shape=[jax.ShapeDtypeStruct((t, d), F32), jax.ShapeDtypeStruct((t, LANES), F32)],
        compiler_params=_params(1), name="mix_ln_route",
    )(x2d, hf, hb, u2d, of, ob, u2d, norm_g.reshape(1, w), w_out_bf, ln_g.reshape(1, d), ln_b.reshape(1, d),
      w_router_t, b_router.reshape(N_EXPERTS, 1))


def _moe_kernel(x_ref, gates_ref, wg_ref, wu_ref, wd_ref, g2_ref, b2_ref, o_ref, xb_scr, acc_scr, *, alpha):
    e = pl.program_id(1)

    @pl.when(e == 0)
    def _():
        xb_scr[...] = x_ref[...].astype(BF16)
        acc_scr[...] = jnp.zeros_like(acc_scr)

    xb = xb_scr[...]
    hid = _silu(jnp.dot(xb, wg_ref[...], preferred_element_type=F32)) * jnp.dot(
        xb, wu_ref[...], preferred_element_type=F32)
    gates = gates_ref[...]
    lane = lax.broadcasted_iota(jnp.int32, gates.shape, 1)
    ge = jnp.sum(jnp.where(lane == e, gates, 0.0), axis=-1, keepdims=True)
    acc_scr[...] += ge * jnp.dot(hid.astype(BF16), wd_ref[...], preferred_element_type=F32)

    @pl.when(e == pl.num_programs(1) - 1)
    def _():
        o_ref[...] = _layer_norm(alpha * x_ref[...] + acc_scr[...], g2_ref[...], b2_ref[...])


def _moe(x2d, gates, wg_bf, wu_bf, wd_bf, ln_g, ln_b, alpha):
    t, d = x2d.shape
    ne, _, dff = wg_bf.shape
    tm = min(MOE_TOKEN_TILE, t)
    in_specs = [
        pl.BlockSpec((tm, d), lambda i, e: (i, 0)),
        pl.BlockSpec((tm, LANES), lambda i, e: (i, 0)),
        pl.BlockSpec((None, d, dff), lambda i, e: (e, 0, 0)),
        pl.BlockSpec((None, d, dff), lambda i, e: (e, 0, 0)),
        pl.BlockSpec((None, dff, d), lambda i, e: (e, 0, 0)),
        pl.BlockSpec((1, d), lambda i, e: (0, 0)),
        pl.BlockSpec((1, d), lambda i, e: (0, 0)),
    ]
    return pl.pallas_call(
        functools.partial(_moe_kernel, alpha=alpha),
        grid=(t // tm, ne), in_specs=in_specs, out_specs=pl.BlockSpec((tm, d), lambda i, e: (i, 0)),
        out_shape=jax.ShapeDtypeStruct((t, d), F32),
        scratch_shapes=[pltpu.VMEM((tm, d), BF16), pltpu.VMEM((tm, d), F32)],
        compiler_params=_params(2), name="moe_ln",
    )(x2d, gates, wg_bf, wu_bf, wd_bf, ln_g.reshape(1, d), ln_b.reshape(1, d))


def _block_diag(blocks):
    nb, bw, _ = blocks.shape
    eye = jnp.eye(nb, dtype=blocks.dtype)
    return jnp.einsum("hij,hg->higj", blocks, eye).reshape(nb * bw, nb * bw)


def _trunk(x, p):
    bsz, n, d = x.shape
    depth = p["w_in"].shape[0]
    alpha = (2 * depth) ** 0.25
    t = bsz * n
    x2d = x.reshape(t, d)
    for l in range(depth):
        ln = (p["ln_in_g"], p["ln_in_b"]) if l == 0 else None
        u2d, x2d = _in_proj(x2d, p["w_in_bf"][l], ln)
        u3 = u2d.reshape(bsz, n, -1)
        hs, os_ = [], []
        for di, rev in enumerate((False, True)):
            hs.append(_lru_scan(u3, p["conv_w"][l], p["conv_b"][l], p["w_gates_bf"][l][di], p["b_gates"][l][di],
                                p["lru_lambda"][l][di], rev))
            os_.append(_hgrn_scan(u3, p["hgrn_lb"][di], l, rev))
        x1, gates = _mix(x2d, hs[0].reshape(t, -1), hs[1].reshape(t, -1), u2d, os_[0].reshape(t, -1),
                         os_[1].reshape(t, -1), p["hgrn_norm_g"][l], p["w_out_bf"][l], p["ln1_g"][l], p["ln1_b"][l],
                         p["w_router_t"], p["b_router"], alpha)
        x2d = _moe(x1, gates, p["w_gate_bf"][l], p["w_up_bf"][l], p["w_down_bf"][l], p["ln2_g"][l], p["ln2_b"][l],
                   alpha)
    return x2d.reshape(bsz, n, d)


def kernel(x_prompt, x_sample, ln_in_g, ln_in_b, w_in, conv_w, conv_b, lru_wa, lru_ba, lru_wx, lru_bx, lru_lambda,
           hgrn_lb, hgrn_norm_g, w_out, ln1_g, ln1_b, w_router, b_router, w_gate, w_up, w_down, ln2_g, ln2_b):
    depth = w_in.shape[0]
    d = w_in.shape[1]
    w_gates = jnp.stack([
        jnp.stack([jnp.concatenate([_block_diag(lru_wa[l, di]), _block_diag(lru_wx[l, di])], axis=1)
                   for di in range(2)]) for l in range(depth)])
    b_gates = jnp.concatenate([lru_ba, lru_bx], axis=-1)
    w_router_t = jnp.zeros((LANES, d), F32).at[:N_EXPERTS].set(w_router.T)
    p = dict(
        ln_in_g=ln_in_g, ln_in_b=ln_in_b, w_in_bf=w_in.astype(BF16), conv_w=conv_w, conv_b=conv_b,
        w_gates_bf=w_gates.astype(BF16), b_gates=b_gates, lru_lambda=lru_lambda, hgrn_lb=hgrn_lb,
        hgrn_norm_g=hgrn_norm_g, w_out_bf=w_out.astype(BF16), ln1_g=ln1_g, ln1_b=ln1_b,
        w_router_t=w_router_t, b_router=b_router, w_gate_bf=w_gate.astype(BF16), w_up_bf=w_up.astype(BF16),
        w_down_bf=w_down.astype(BF16), ln2_g=ln2_g, ln2_b=ln2_b, w_in=w_in)
    return (_trunk(x_prompt, p), _trunk(x_sample, p))
```

```python
import functools

import jax
import jax.numpy as jnp
from jax import lax
from jax.experimental import pallas as pl
from jax.experimental.pallas import tpu as pltpu

F32 = jnp.float32
BF16 = jnp.bfloat16

LRU_WIDTH = 512
LRU_BLOCKS = 8
CONV_W = 4
CONV_LEFT = 2
LRU_C = 8.0
HG_HEADS = 4
HG_DK = 128
HG_DV = 128
HG_W = HG_HEADS * HG_DK
N_EXPERTS = 16
N_GROUPS = 4
EXPERTS_PER_GROUP = N_EXPERTS // N_GROUPS
PAIR_LO = (0, 0, 0, 1, 1, 2)
PAIR_HI = (1, 2, 3, 2, 3, 3)
N_PAIRS = len(PAIR_LO)
N_CLASSES = N_GROUPS * N_PAIRS
CLASS_ROWS = 32
LN_EPS = 1e-5
RMS_EPS = 1e-6
F32_TINY = 1.1754944e-38
COL_LRU_X, COL_LRU_GATE, COL_Q, COL_I, COL_F_FWD, COL_F_BWD, COL_G = range(7)

LANES = 128
SUBLANES = 8
VMEM_LIMIT_BYTES = 56 * 1024 * 1024

TOKEN_TILE = 512
LRU_TIME_TILE = 512
HG_TIME_TILE = 256
HG_CHUNK = 64
MOE_ROW_TILE = 256
HG_EXP_CLAMP = 80.0


def _params(n_axes):
    return pltpu.CompilerParams(dimension_semantics=("arbitrary",) * n_axes,
                                vmem_limit_bytes=VMEM_LIMIT_BYTES)


def _layer_norm(x, g, b):
    mu = jnp.mean(x, axis=-1, keepdims=True)
    xc = x - mu
    var = jnp.mean(xc * xc, axis=-1, keepdims=True)
    return xc * lax.rsqrt(var + LN_EPS) * g + b


def _sigmoid(x):
    return 0.5 + 0.5 * jnp.tanh(0.5 * x)


def _silu(x):
    return x * _sigmoid(x)


def _in_proj_ln_kernel(x_ref, g_ref, b_ref, w_ref, u_ref, xn_ref):
    xn = _layer_norm(x_ref[...], g_ref[...], b_ref[...])
    xn_ref[...] = xn
    u_ref[...] = jnp.dot(xn.astype(BF16), w_ref[...], preferred_element_type=F32)


def _in_proj_kernel(x_ref, w_ref, u_ref):
    u_ref[...] = jnp.dot(x_ref[...].astype(BF16), w_ref[...], preferred_element_type=F32)


def _in_proj(x2d, w_in_bf, ln=None):
    t, d = x2d.shape
    cols = w_in_bf.shape[1]
    tm = min(TOKEN_TILE, t)
    grid = (t // tm,)
    x_spec = pl.BlockSpec((tm, d), lambda i: (i, 0))
    w_spec = pl.BlockSpec((d, cols), lambda i: (0, 0))
    u_spec = pl.BlockSpec((tm, cols), lambda i: (i, 0))
    vec_spec = pl.BlockSpec((1, d), lambda i: (0, 0))
    if ln is None:
        u = pl.pallas_call(
            _in_proj_kernel, grid=grid, in_specs=[x_spec, w_spec], out_specs=u_spec,
            out_shape=jax.ShapeDtypeStruct((t, cols), F32), compiler_params=_params(1),
            name="in_proj")(x2d, w_in_bf)
        return u, x2d
    g, b = ln
    u, xn = pl.pallas_call(
        _in_proj_ln_kernel, grid=grid, in_specs=[x_spec, vec_spec, vec_spec, w_spec],
        out_specs=[u_spec, x_spec],
        out_shape=[jax.ShapeDtypeStruct((t, cols), F32), jax.ShapeDtypeStruct((t, d), F32)],
        compiler_params=_params(1), name="in_proj_ln")(x2d, g.reshape(1, d), b.reshape(1, d), w_in_bf)
    return u, xn


def _lru_kernel(cur_ref, prev_ref, next_ref, cw_ref, cb_ref, wg_ref, bg_ref, lam_ref, h_ref,
                a_scr, u_scr, cin_scr, c_scr, *, reverse, tc):
    t = pl.program_id(1)
    nt = pl.num_programs(1)
    tt = (nt - 1 - t) if reverse else t
    w = LRU_WIDTH
    ng = tc // SUBLANES

    @pl.when(t == 0)
    def _():
        c_scr[...] = jnp.zeros_like(c_scr)

    cur = cur_ref[...].reshape(ng, SUBLANES, w)
    prev = jnp.where(tt > 0, prev_ref[...], 0.0)
    nxt = jnp.where(tt < nt - 1, next_ref[...], 0.0)
    rm = lax.broadcasted_iota(jnp.int32, (ng, SUBLANES, w), 1)

    def shift_down(k):
        s = pltpu.roll(cur, k, axis=1)
        before = jnp.concatenate([pltpu.roll(prev, k, axis=0)[None], s[:ng - 1]], axis=0)
        return jnp.where(rm < k, before, s)

    s_up = pltpu.roll(cur, SUBLANES - 1, axis=1)
    after = jnp.concatenate([s_up[1:], pltpu.roll(nxt, SUBLANES - 1, axis=0)[None]], axis=0)
    x_p1 = jnp.where(rm == SUBLANES - 1, after, s_up)

    cw = cw_ref[...]
    xc = shift_down(2) * cw[0:1] + shift_down(1) * cw[1:2]
    xc = xc + cur * cw[2:3]
    xc = xc + x_p1 * cw[3:4]
    xc = (xc + cb_ref[...]).reshape(tc, w)

    z = jnp.dot(xc.astype(BF16), wg_ref[...], preferred_element_type=F32) + bg_ref[...]
    r = _sigmoid(z[:, :w])
    ig = _sigmoid(z[:, w:])
    lam = lam_ref[...]
    e = jnp.exp(-jnp.abs(lam))
    e1 = 1.0 + e
    d = e1 - 1.0
    log1p_e = jnp.where(d == 0.0, e, jnp.log(e1) * (e / jnp.where(d == 0.0, 1.0, d)))
    log_a = (LRU_C * (jnp.minimum(lam, 0.0) - log1p_e)) * r
    a = jnp.exp(log_a)
    z1 = 1.0 - a * a
    u = (z1 * lax.rsqrt(jnp.maximum(z1, F32_TINY))) * (ig * xc)

    a = a.reshape(ng, SUBLANES, w)
    u = u.reshape(ng, SUBLANES, w)
    for k in (1, 2, 4):
        if reverse:
            keep = rm < SUBLANES - k
            shift = SUBLANES - k
        else:
            keep = rm >= k
            shift = k
        a_s = jnp.where(keep, pltpu.roll(a, shift, axis=1), 1.0)
        u_s = jnp.where(keep, pltpu.roll(u, shift, axis=1), 0.0)
        u = u + a * u_s
        a = a * a_s
    a_scr[...] = a.reshape(tc, w)
    u_scr[...] = u.reshape(tc, w)

    edge = 0 if reverse else SUBLANES - 1

    def chain(i, c):
        g = (ng - 1 - i) if reverse else i
        cin_scr[pl.ds(g, 1), :] = c
        r = g * SUBLANES + edge
        return u_scr[pl.ds(r, 1), :] + a_scr[pl.ds(r, 1), :] * c

    c_scr[...] = lax.fori_loop(0, ng, chain, c_scr[...], unroll=8)

    def apply(g, carry):
        r0 = pl.multiple_of(g * SUBLANES, SUBLANES)
        h_ref[pl.ds(r0, SUBLANES), :] = (u_scr[pl.ds(r0, SUBLANES), :]
                                         + a_scr[pl.ds(r0, SUBLANES), :] * cin_scr[pl.ds(g, 1), :])
        return carry

    lax.fori_loop(0, ng, apply, 0, unroll=8)


def _lru_scan(u3, conv_w, conv_b, w_gates_bf, b_gates, lam, reverse):
    bsz, n, _ = u3.shape
    w = LRU_WIDTH
    tc = min(LRU_TIME_TILE, n)
    nt = n // tc
    nb8 = n // SUBLANES
    per = tc // SUBLANES

    def tmap(t):
        return (nt - 1 - t) if reverse else t

    in_specs = [
        pl.BlockSpec((None, tc, w), lambda b, t: (b, tmap(t), COL_LRU_X)),
        pl.BlockSpec((None, SUBLANES, w), lambda b, t: (b, jnp.maximum(tmap(t) * per - 1, 0), COL_LRU_X)),
        pl.BlockSpec((None, SUBLANES, w), lambda b, t: (b, jnp.minimum((tmap(t) + 1) * per, nb8 - 1), COL_LRU_X)),
        pl.BlockSpec((CONV_W, w), lambda b, t: (0, 0)),
        pl.BlockSpec((1, w), lambda b, t: (0, 0)),
        pl.BlockSpec((w, 2 * w), lambda b, t: (0, 0)),
        pl.BlockSpec((1, 2 * w), lambda b, t: (0, 0)),
        pl.BlockSpec((1, w), lambda b, t: (0, 0)),
    ]
    out_spec = pl.BlockSpec((None, tc, w), lambda b, t: (b, tmap(t), 0))
    return pl.pallas_call(
        functools.partial(_lru_kernel, reverse=reverse, tc=tc),
        grid=(bsz, nt), in_specs=in_specs, out_specs=out_spec,
        out_shape=jax.ShapeDtypeStruct((bsz, n, w), F32),
        scratch_shapes=[pltpu.VMEM((tc, w), F32), pltpu.VMEM((tc, w), F32), pltpu.VMEM((tc // SUBLANES, w), F32),
                        pltpu.VMEM((1, w), F32)],
        compiler_params=_params(2), name="lru_bwd" if reverse else "lru_fwd",
    )(u3, u3, u3, conv_w, conv_b.reshape(1, w), w_gates_bf, b_gates.reshape(1, 2 * w), lam.reshape(1, w))


def _hgrn_kernel(q_ref, v_ref, f_ref, lb_ref, o_ref, st_scr, *, reverse, layer, tb, c):
    t = pl.program_id(1)

    @pl.when(t == 0)
    def _():
        st_scr[...] = jnp.zeros_like(st_scr)

    z = lb_ref[...]
    e = jnp.exp(z - jnp.max(z, axis=0, keepdims=True))
    p = e / jnp.sum(e, axis=0, keepdims=True)
    lb_all = jnp.sum(p[0:layer + 1], axis=0, keepdims=True) - p[0:1]

    ri = lax.broadcasted_iota(jnp.int32, (c, c), 0)
    ci = lax.broadcasted_iota(jnp.int32, (c, c), 1)
    valid = (ci >= ri) if reverse else (ci <= ri)
    tri = jnp.where(valid, 1.0, 0.0).astype(BF16)
    half = c // 2
    scale = HG_DK ** -0.5
    nchunks = tb // c
    nt_dims = (((1,), (1,)), ((), ()))
    tn_dims = (((0,), (0,)), ((), ()))

    order = [(j, h) for j in (range(nchunks - 1, -1, -1) if reverse else range(nchunks)) for h in range(HG_HEADS)]
    q_all, k_all, v_all, cum_all = {}, {}, {}, {}
    for j, h in order:
        sl = slice(j * c, (j + 1) * c)
        hl = slice(h * HG_DK, (h + 1) * HG_DK)
        lb = lb_all[:, hl]
        q_all[j, h] = _silu(q_ref[sl, hl]) * scale
        v_all[j, h] = v_ref[sl, hl].astype(BF16)
        f = lb + (1.0 - lb) * _sigmoid(f_ref[sl, hl])
        k_all[j, h] = 1.0 - f
        g = jnp.log(f)
        g_hi = g.astype(BF16)
        r1 = g - g_hi.astype(F32)
        g_mid = r1.astype(BF16)
        g_lo = (r1 - g_mid.astype(F32)).astype(BF16)
        cum3 = jnp.dot(tri, jnp.concatenate([g_hi, g_mid, g_lo], axis=1), preferred_element_type=F32)
        cum_all[j, h] = (cum3[:, 0:HG_DK] + cum3[:, HG_DK:2 * HG_DK]) + cum3[:, 2 * HG_DK:3 * HG_DK]

    scores_all, qs_all, ds_all, dec_all = {}, {}, {}, {}
    for j, h in order:
        q, k, cum = q_all[j, h], k_all[j, h], cum_all[j, h]
        if reverse:
            mid = cum[half:half + 1]
            tot = cum[0:1]
        else:
            mid = cum[half - 1:half]
            tot = cum[c - 1:c]
        qt = q * jnp.exp(jnp.minimum(cum - mid, HG_EXP_CLAMP))
        kt = k * jnp.exp(jnp.minimum(mid - cum, HG_EXP_CLAMP))
        scores = lax.dot_general(qt.astype(BF16), kt.astype(BF16), nt_dims, preferred_element_type=F32)
        scores_all[j, h] = jnp.where(valid, scores, 0.0).astype(BF16)
        qs_all[j, h] = (q * jnp.exp(cum)).astype(BF16)
        ks = (k * jnp.exp(tot - cum)).astype(BF16)
        ds_all[j, h] = lax.dot_general(v_all[j, h], ks, tn_dims, preferred_element_type=F32)
        dec_all[j, h] = jnp.exp(tot)

    intra_all = {}
    for j, h in order:
        intra_all[j, h] = jnp.dot(scores_all[j, h], v_all[j, h], preferred_element_type=F32)

    states = [st_scr[h] for h in range(HG_HEADS)]
    for j, h in order:
        st = states[h]
        o = intra_all[j, h] + lax.dot_general(qs_all[j, h], st.astype(BF16), nt_dims, preferred_element_type=F32)
        o_ref[j * c:(j + 1) * c, h * HG_DV:(h + 1) * HG_DV] = o
        states[h] = st * dec_all[j, h] + ds_all[j, h]
    for h in range(HG_HEADS):
        st_scr[h] = states[h]


def _hgrn_scan(u3, lb_raw, layer, reverse):
    bsz, n, _ = u3.shape
    depth = lb_raw.shape[0]
    tb = min(HG_TIME_TILE, n)
    nt = n // tb
    c = min(HG_CHUNK, tb)
    assert n % tb == 0 and tb % c == 0
    f_col = COL_F_BWD if reverse else COL_F_FWD

    def tmap(t):
        return (nt - 1 - t) if reverse else t

    in_specs = [
        pl.BlockSpec((None, tb, HG_W), lambda b, t: (b, tmap(t), COL_Q)),
        pl.BlockSpec((None, tb, HG_W), lambda b, t: (b, tmap(t), COL_I)),
        pl.BlockSpec((None, tb, HG_W), lambda b, t: (b, tmap(t), f_col)),
        pl.BlockSpec((depth, HG_W), lambda b, t: (0, 0)),
    ]
    out_spec = pl.BlockSpec((None, tb, HG_W), lambda b, t: (b, tmap(t), 0))
    return pl.pallas_call(
        functools.partial(_hgrn_kernel, reverse=reverse, layer=layer, tb=tb, c=c),
        grid=(bsz, nt), in_specs=in_specs, out_specs=out_spec,
        out_shape=jax.ShapeDtypeStruct((bsz, n, HG_W), F32),
        scratch_shapes=[pltpu.VMEM((HG_HEADS, HG_DV, HG_DK), F32)],
        compiler_params=_params(2), name="hgrn_bwd" if reverse else "hgrn_fwd",
    )(u3, u3, u3, lb_raw)


def _route(logits_t, b_router):
    mx = jnp.max(logits_t, axis=0, keepdims=True)
    ex = jnp.exp(logits_t - mx)
    probs = ex / jnp.sum(ex, axis=0, keepdims=True)
    sel = probs + b_router
    rows = [sel[i:i + 1] for i in range(N_EXPERTS)]
    prow = [probs[i:i + 1] for i in range(N_EXPERTS)]
    gscore = []
    for gi in range(N_GROUPS):
        m = rows[gi * EXPERTS_PER_GROUP:(gi + 1) * EXPERTS_PER_GROUP]
        best = None
        for a in range(EXPERTS_PER_GROUP):
            for b in range(a + 1, EXPERTS_PER_GROUP):
                s = m[a] + m[b]
                best = s if best is None else jnp.maximum(best, s)
        gscore.append(best)
    gbest = jnp.zeros_like(gscore[0], dtype=jnp.int32)
    gmax = gscore[0]
    for gi in range(1, N_GROUPS):
        better = gscore[gi] > gmax
        gbest = jnp.where(better, gi, gbest)
        gmax = jnp.where(better, gscore[gi], gmax)
    ms, mp = [], []
    for j in range(EXPERTS_PER_GROUP):
        s = rows[j]
        pr = prow[j]
        for gi in range(1, N_GROUPS):
            pick = gbest == gi
            s = jnp.where(pick, rows[gi * EXPERTS_PER_GROUP + j], s)
            pr = jnp.where(pick, prow[gi * EXPERTS_PER_GROUP + j], pr)
        ms.append(s)
        mp.append(pr)
    i1 = jnp.zeros_like(gbest)
    v1 = ms[0]
    for j in range(1, EXPERTS_PER_GROUP):
        better = ms[j] > v1
        i1 = jnp.where(better, j, i1)
        v1 = jnp.where(better, ms[j], v1)
    i2 = jnp.full_like(gbest, -1)
    v2 = jnp.full_like(v1, -jnp.inf)
    for j in range(EXPERTS_PER_GROUP):
        better = (i1 != j) & ((ms[j] > v2) | (i2 < 0))
        i2 = jnp.where(better, j, i2)
        v2 = jnp.where(better, ms[j], v2)
    p1 = mp[0]
    p2 = mp[0]
    for j in range(1, EXPERTS_PER_GROUP):
        p1 = jnp.where(i1 == j, mp[j], p1)
        p2 = jnp.where(i2 == j, mp[j], p2)
    den = p1 + p2
    w1 = p1 / den
    w2 = p2 / den
    first_low = i1 < i2
    lo = jnp.where(first_low, i1, i2)
    hi = jnp.where(first_low, i2, i1)
    pair = jnp.where(lo == 0, hi - 1, jnp.where(lo == 1, hi + 1, 5))
    cls = gbest * N_PAIRS + pair
    return cls, jnp.where(first_low, w1, w2), jnp.where(first_low, w2, w1)


def _mix_kernel(x_ref, hf_ref, hb_ref, ug_ref, of_ref, ob_ref, uhg_ref, ng_ref, wo_ref, g1_ref, b1_ref,
                wr_ref, br_ref, x1w_ref, cls_ref, rank_ref, cnt_ref, carry_scr, *, alpha):
    tm = x_ref.shape[0]

    @pl.when(pl.program_id(0) == 0)
    def _():
        carry_scr[...] = jnp.zeros_like(carry_scr)

    lru = (hf_ref[...] + hb_ref[...]) * jax.nn.gelu(ug_ref[...], approximate=True)
    o = of_ref[...] + ob_ref[...]
    heads = []
    for h in range(HG_HEADS):
        oh = o[:, h * HG_DV:(h + 1) * HG_DV]
        heads.append(oh * lax.rsqrt(jnp.mean(oh * oh, axis=-1, keepdims=True) + RMS_EPS))
    hg = jnp.concatenate(heads, axis=1) * ng_ref[...] * _silu(uhg_ref[...])
    mix_in = jnp.concatenate([lru, hg], axis=1).astype(BF16)
    mix = jnp.dot(mix_in, wo_ref[...], preferred_element_type=F32)
    x1 = _layer_norm(alpha * x_ref[...] + mix, g1_ref[...], b1_ref[...])
    logits_t = lax.dot_general(wr_ref[...], x1, (((1,), (1,)), ((), ())),
                               precision=lax.Precision.HIGHEST, preferred_element_type=F32)
    cls, w_lo, w_hi = _route(logits_t[0:N_EXPERTS], br_ref[...])
    rid = lax.broadcasted_iota(jnp.int32, (LANES, tm), 0)
    route_t = jnp.where(rid == 0, w_lo, jnp.where(rid == 1, w_hi, 0.0))
    x1w_ref[...] = jnp.concatenate([x1, route_t.T], axis=1)
    cid = lax.broadcasted_iota(jnp.int32, (CLASS_ROWS, tm), 0)
    onehot = jnp.where(cid == cls, 1.0, 0.0)
    si = lax.broadcasted_iota(jnp.int32, (tm, tm), 0)
    ti = lax.broadcasted_iota(jnp.int32, (tm, tm), 1)
    upper = jnp.where(si <= ti, 1.0, 0.0).astype(BF16)
    prefix = jnp.dot(onehot.astype(BF16), upper, preferred_element_type=F32)
    carry = carry_scr[...]
    rank = jnp.sum(onehot * (prefix - 1.0 + carry), axis=0, keepdims=True)
    cls_ref[...] = cls
    rank_ref[...] = rank.astype(jnp.int32)
    carry = carry + prefix[:, tm - 1:tm]
    carry_scr[...] = carry
    cnt_ref[...] = jnp.broadcast_to(carry, cnt_ref.shape)


def _mix(x2d, hf, hb, u2d, of, ob, norm_g, w_out_bf, ln_g, ln_b, w_router_t, b_router, alpha):
    t, d = x2d.shape
    w = LRU_WIDTH
    tm = min(TOKEN_TILE, t)
    assert t % tm == 0
    nt = t // tm
    row = lambda i: (i, 0)
    const = lambda i: (0, 0)
    in_specs = [
        pl.BlockSpec((tm, d), row),
        pl.BlockSpec((tm, w), row),
        pl.BlockSpec((tm, w), row),
        pl.BlockSpec((tm, w), lambda i: (i, COL_LRU_GATE)),
        pl.BlockSpec((tm, w), row),
        pl.BlockSpec((tm, w), row),
        pl.BlockSpec((tm, w), lambda i: (i, COL_G)),
        pl.BlockSpec((1, w), const),
        pl.BlockSpec((2 * w, d), const),
        pl.BlockSpec((1, d), const),
        pl.BlockSpec((1, d), const),
        pl.BlockSpec((LANES, d), const),
        pl.BlockSpec((N_EXPERTS, 1), const),
    ]
    tok_spec = pl.BlockSpec((None, 1, tm), lambda i: (i, 0, 0))
    out_specs = [pl.BlockSpec((tm, d + LANES), row), tok_spec, tok_spec, pl.BlockSpec((CLASS_ROWS, LANES), const)]
    x1w, cls, rank, cnt = pl.pallas_call(
        functools.partial(_mix_kernel, alpha=alpha),
        grid=(nt,), in_specs=in_specs, out_specs=out_specs,
        out_shape=[jax.ShapeDtypeStruct((t, d + LANES), F32), jax.ShapeDtypeStruct((nt, 1, tm), jnp.int32),
                   jax.ShapeDtypeStruct((nt, 1, tm), jnp.int32), jax.ShapeDtypeStruct((CLASS_ROWS, LANES), F32)],
        scratch_shapes=[pltpu.VMEM((CLASS_ROWS, 1), F32)],
        compiler_params=_params(1), name="mix_ln_route",
    )(x2d, hf, hb, u2d, of, ob, u2d, norm_g.reshape(1, w), w_out_bf, ln_g.reshape(1, d), ln_b.reshape(1, d),
      w_router_t, b_router.reshape(N_EXPERTS, 1))
    return x1w, cls.reshape(t), rank.reshape(t), cnt[:N_CLASSES, 0].astype(jnp.int32)


def _route_plan(cls, rank, counts, t, tm):
    nt = t // tm + N_CLASSES
    tiles_c = (counts + tm - 1) // tm
    tile_end = jnp.cumsum(tiles_c)
    tile_start = tile_end - tiles_c
    pos = (tile_start * tm)[cls] + rank
    src = jnp.zeros((nt * tm,), jnp.int32).at[pos].set(jnp.arange(t, dtype=jnp.int32))
    tile = jnp.arange(nt, dtype=jnp.int32)
    tile_cls = jnp.searchsorted(tile_end, jnp.minimum(tile, tile_end[-1] - 1), side="right").astype(jnp.int32)
    nvalid = jnp.clip(counts[tile_cls] - (tile - tile_start[tile_cls]) * tm, 0, tm)
    nvalid = jnp.where(tile < tile_end[-1], nvalid, 0).astype(jnp.int32)
    group = tile_cls // N_PAIRS
    pair = tile_cls % N_PAIRS
    ea = group * EXPERTS_PER_GROUP + jnp.array(PAIR_LO, jnp.int32)[pair]
    eb = group * EXPERTS_PER_GROUP + jnp.array(PAIR_HI, jnp.int32)[pair]
    return src.reshape(nt, 1, tm), ea, eb, nvalid


def _moe_kernel(ea_ref, eb_ref, nv_ref, src_ref, srcn_ref, x_hbm, wga_ref, wua_ref, wda_ref, wgb_ref, wub_ref,
                wdb_ref, y_hbm, xbuf, ybuf, sem_in, sem_out, *, tm, d):
    del ea_ref, eb_ref
    i = pl.program_id(0)
    nt = pl.num_programs(0)
    slot = lax.rem(i, 2)

    def gather_copy(tok, r, s):
        return pltpu.make_async_copy(x_hbm.at[pl.ds(tok, 1)], xbuf.at[s, pl.ds(r, 1)], sem_in.at[s])

    def scatter_copy(row, r, s):
        return pltpu.make_async_copy(ybuf.at[s, pl.ds(r, 1)], y_hbm.at[pl.ds(row, 1)], sem_out.at[s])

    def start_gather(idx_ref, s):
        def body(r, carry):
            gather_copy(idx_ref[0, r], r, s).start()
            return carry
        lax.fori_loop(0, tm, body, 0, unroll=8)

    def wait_gather(s):
        pltpu.make_async_copy(x_hbm.at[pl.ds(0, tm)], xbuf.at[s], sem_in.at[s]).wait()

    def wait_scatter(s, n):
        n8 = pl.multiple_of(lax.shift_left(lax.shift_right_logical(n, 3), 3), SUBLANES)

        @pl.when(n8 > 0)
        def _():
            pltpu.make_async_copy(ybuf.at[s, pl.ds(0, n8)], y_hbm.at[pl.ds(0, n8)], sem_out.at[s]).wait()

        for part in (1, 2, 4):
            @pl.when((n & part) != 0)
            def _():
                pltpu.make_async_copy(ybuf.at[s, pl.ds(0, part)], y_hbm.at[pl.ds(0, part)], sem_out.at[s]).wait()

    @pl.when(i == 0)
    def _():
        start_gather(src_ref, 0)

    @pl.when(i + 1 < nt)
    def _():
        start_gather(srcn_ref, 1 - slot)

    wait_gather(slot)

    @pl.when(i >= 2)
    def _():
        wait_scatter(slot, nv_ref[jnp.maximum(i - 2, 0)])

    xw = xbuf[slot]
    x = xw[:, :d].astype(BF16)
    w_lo = xw[:, d:d + 1]
    w_hi = xw[:, d + 1:d + 2]

    def expert(wg_ref, wu_ref, wd_ref):
        hid = _silu(jnp.dot(x, wg_ref[...], preferred_element_type=F32)) * jnp.dot(
            x, wu_ref[...], preferred_element_type=F32)
        return jnp.dot(hid.astype(BF16), wd_ref[...], preferred_element_type=F32)

    ybuf[slot] = w_lo * expert(wga_ref, wua_ref, wda_ref) + w_hi * expert(wgb_ref, wub_ref, wdb_ref)

    nv = nv_ref[i]

    def body(r, carry):
        scatter_copy(src_ref[0, r], r, slot).start()
        return carry
    lax.fori_loop(0, nv, body, 0)

    @pl.when(i == nt - 1)
    def _():
        wait_scatter(slot, nv)

        @pl.when(nt >= 2)
        def _():
            wait_scatter(1 - slot, nv_ref[jnp.maximum(i - 1, 0)])


def _moe(x1w, src, ea, eb, nvalid, wg_bf, wu_bf, wd_bf):
    t = x1w.shape[0]
    d = x1w.shape[1] - LANES
    _, _, dff = wg_bf.shape
    nt, _, tm = src.shape
    idx_spec = pl.BlockSpec((None, 1, tm), lambda i, ea, eb, nv: (i, 0, 0), memory_space=pltpu.SMEM)
    idx_next_spec = pl.BlockSpec((None, 1, tm), lambda i, ea, eb, nv: (jnp.minimum(i + 1, nt - 1), 0, 0),
                                 memory_space=pltpu.SMEM)
    any_spec = pl.BlockSpec(memory_space=pl.ANY)
    up_a = pl.BlockSpec((None, d, dff), lambda i, ea, eb, nv: (ea[i], 0, 0))
    down_a = pl.BlockSpec((None, dff, d), lambda i, ea, eb, nv: (ea[i], 0, 0))
    up_b = pl.BlockSpec((None, d, dff), lambda i, ea, eb, nv: (eb[i], 0, 0))
    down_b = pl.BlockSpec((None, dff, d), lambda i, ea, eb, nv: (eb[i], 0, 0))
    grid_spec = pltpu.PrefetchScalarGridSpec(
        num_scalar_prefetch=3, grid=(nt,),
        in_specs=[idx_spec, idx_next_spec, any_spec, up_a, up_a, down_a, up_b, up_b, down_b],
        out_specs=any_spec,
        scratch_shapes=[pltpu.VMEM((2, tm, d + LANES), F32), pltpu.VMEM((2, tm, d), F32),
                        pltpu.SemaphoreType.DMA((2,)), pltpu.SemaphoreType.DMA((2,))])
    return pl.pallas_call(
        functools.partial(_moe_kernel, tm=tm, d=d), grid_spec=grid_spec,
        out_shape=jax.ShapeDtypeStruct((t, d), F32),
        compiler_params=_params(1), name="moe_routed",
    )(ea, eb, nvalid, src, src, x1w, wg_bf, wu_bf, wd_bf, wg_bf, wu_bf, wd_bf)


def _ln2_in_proj_kernel(x1_ref, y_ref, g_ref, b_ref, w_ref, u_ref, x2_ref, *, alpha):
    x2 = _layer_norm(alpha * x1_ref[...] + y_ref[...], g_ref[...], b_ref[...])
    x2_ref[...] = x2
    u_ref[...] = jnp.dot(x2.astype(BF16), w_ref[...], preferred_element_type=F32)


def _ln2_kernel(x1_ref, y_ref, g_ref, b_ref, x2_ref, *, alpha):
    x2_ref[...] = _layer_norm(alpha * x1_ref[...] + y_ref[...], g_ref[...], b_ref[...])


def _ln2(x1w, y, ln_g, ln_b, alpha, t, w_in_bf=None):
    d = y.shape[1]
    tm = min(TOKEN_TILE, t)
    assert t % tm == 0
    row = lambda i: (i, 0)
    const = lambda i: (0, 0)
    x_spec = pl.BlockSpec((tm, d), row)
    vec_spec = pl.BlockSpec((1, d), const)
    if w_in_bf is None:
        return pl.pallas_call(
            functools.partial(_ln2_kernel, alpha=alpha), grid=(t // tm,),
            in_specs=[x_spec, x_spec, vec_spec, vec_spec], out_specs=x_spec,
            out_shape=jax.ShapeDtypeStruct((t, d), F32), compiler_params=_params(1), name="ln2",
        )(x1w, y, ln_g.reshape(1, d), ln_b.reshape(1, d))
    cols = w_in_bf.shape[1]
    return pl.pallas_call(
        functools.partial(_ln2_in_proj_kernel, alpha=alpha), grid=(t // tm,),
        in_specs=[x_spec, x_spec, vec_spec, vec_spec, pl.BlockSpec((d, cols), const)],
        out_specs=[pl.BlockSpec((tm, cols), row), x_spec],
        out_shape=[jax.ShapeDtypeStruct((t, cols), F32), jax.ShapeDtypeStruct((t, d), F32)],
        compiler_params=_params(1), name="ln2_in_proj",
    )(x1w, y, ln_g.reshape(1, d), ln_b.reshape(1, d), w_in_bf)


def _block_diag(blocks):
    nb, bw, _ = blocks.shape
    eye = jnp.eye(nb, dtype=blocks.dtype)
    return jnp.einsum("hij,hg->higj", blocks, eye).reshape(nb * bw, nb * bw)


def _trunk(x, p):
    bsz, n, d = x.shape
    depth = p["w_in"].shape[0]
    alpha = (2 * depth) ** 0.25
    t = bsz * n
    u2d, x2d = _in_proj(x.reshape(t, d), p["w_in_bf"][0], (p["ln_in_g"], p["ln_in_b"]))
    for l in range(depth):
        u3 = u2d.reshape(bsz, n, -1)
        hs, os_ = [], []
        for di, rev in enumerate((False, True)):
            hs.append(_lru_scan(u3, p["conv_w"][l], p["conv_b"][l], p["w_gates_bf"][l][di], p["b_gates"][l][di],
                                p["lru_lambda"][l][di], rev))
            os_.append(_hgrn_scan(u3, p["hgrn_lb"][di], l, rev))
        x1w, cls, rank, counts = _mix(
            x2d, hs[0].reshape(t, -1), hs[1].reshape(t, -1), u2d, os_[0].reshape(t, -1), os_[1].reshape(t, -1),
            p["hgrn_norm_g"][l], p["w_out_bf"][l], p["ln1_g"][l], p["ln1_b"][l], p["w_router_t"], p["b_router"], alpha)
        src, ea, eb, nvalid = _route_plan(cls, rank, counts, t, min(MOE_ROW_TILE, t))
        y = _moe(x1w, src, ea, eb, nvalid, p["w_gate_bf"][l], p["w_up_bf"][l], p["w_down_bf"][l])
        if l + 1 < depth:
            u2d, x2d = _ln2(x1w, y, p["ln2_g"][l], p["ln2_b"][l], alpha, t, p["w_in_bf"][l + 1])
        else:
            x2d = _ln2(x1w, y, p["ln2_g"][l], p["ln2_b"][l], alpha, t)
    return x2d.reshape(bsz, n, d)


def kernel(x_prompt, x_sample, ln_in_g, ln_in_b, w_in, conv_w, conv_b, lru_wa, lru_ba, lru_wx, lru_bx, lru_lambda,
           hgrn_lb, hgrn_norm_g, w_out, ln1_g, ln1_b, w_router, b_router, w_gate, w_up, w_down, ln2_g, ln2_b):
    depth = w_in.shape[0]
    d = w_in.shape[1]
    w_gates = jnp.stack([
        jnp.stack([jnp.concatenate([_block_diag(lru_wa[l, di]), _block_diag(lru_wx[l, di])], axis=1)
                   for di in range(2)]) for l in range(depth)])
    b_gates = jnp.concatenate([lru_ba, lru_bx], axis=-1)
    w_router_t = jnp.zeros((LANES, d), F32).at[:N_EXPERTS].set(w_router.T)
    p = dict(
        ln_in_g=ln_in_g, ln_in_b=ln_in_b, w_in_bf=w_in.astype(BF16), conv_w=conv_w, conv_b=conv_b,
        w_gates_bf=w_gates.astype(BF16), b_gates=b_gates, lru_lambda=lru_lambda, hgrn_lb=hgrn_lb,
        hgrn_norm_g=hgrn_norm_g, w_out_bf=w_out.astype(BF16), ln1_g=ln1_g, ln1_b=ln1_b,
        w_router_t=w_router_t, b_router=b_router, w_gate_bf=w_gate.astype(BF16), w_up_bf=w_up.astype(BF16),
        w_down_bf=w_down.astype(BF16), ln2_g=ln2_g, ln2_b=ln2_b, w_in=w_in)
    return (_trunk(x_prompt, p), _trunk(x_sample, p))
```

```python
import functools

import jax
import jax.numpy as jnp
from jax import lax
from jax.experimental import pallas as pl
from jax.experimental.pallas import tpu as pltpu

F32 = jnp.float32
BF16 = jnp.bfloat16

LRU_WIDTH = 512
LRU_BLOCKS = 8
CONV_W = 4
CONV_LEFT = 2
LRU_C = 8.0
HG_HEADS = 4
HG_DK = 128
HG_DV = 128
HG_W = HG_HEADS * HG_DK
N_EXPERTS = 16
N_GROUPS = 4
EXPERTS_PER_GROUP = N_EXPERTS // N_GROUPS
PAIR_LO = (0, 0, 0, 1, 1, 2)
PAIR_HI = (1, 2, 3, 2, 3, 3)
N_PAIRS = len(PAIR_LO)
N_CLASSES = N_GROUPS * N_PAIRS
CLASS_ROWS = 32
LN_EPS = 1e-5
RMS_EPS = 1e-6
F32_TINY = 1.1754944e-38
LOG2_E = 1.4426950408889634
COL_LRU_X, COL_LRU_GATE, COL_Q, COL_I, COL_F_FWD, COL_F_BWD, COL_G = range(7)

LANES = 128
SUBLANES = 8
VMEM_LIMIT_BYTES = 56 * 1024 * 1024

TOKEN_TILE = 512
LRU_TIME_TILE = 512
HG_TIME_TILE = 512
HG_CHUNK = 64
MOE_ROW_TILE = 256
HG_EXP2_CLAMP = 115.0


def _params(n_axes):
    return pltpu.CompilerParams(dimension_semantics=("arbitrary",) * n_axes,
                                vmem_limit_bytes=VMEM_LIMIT_BYTES)


def _layer_norm(x, g, b):
    mu = jnp.mean(x, axis=-1, keepdims=True)
    xc = x - mu
    var = jnp.mean(xc * xc, axis=-1, keepdims=True)
    return xc * lax.rsqrt(var + LN_EPS) * g + b


def _sigmoid(x):
    return 0.5 + 0.5 * jnp.tanh(0.5 * x)


def _silu(x):
    return x * _sigmoid(x)


def _in_proj_ln_kernel(x_ref, g_ref, b_ref, w_ref, u_ref, xn_ref):
    xn = _layer_norm(x_ref[...], g_ref[...], b_ref[...])
    xn_ref[...] = xn
    u_ref[...] = jnp.dot(xn.astype(BF16), w_ref[...], preferred_element_type=F32)


def _in_proj_kernel(x_ref, w_ref, u_ref):
    u_ref[...] = jnp.dot(x_ref[...].astype(BF16), w_ref[...], preferred_element_type=F32)


def _in_proj(x2d, w_in_bf, ln=None):
    t, d = x2d.shape
    cols = w_in_bf.shape[1]
    tm = min(TOKEN_TILE, t)
    grid = (t // tm,)
    x_spec = pl.BlockSpec((tm, d), lambda i: (i, 0))
    w_spec = pl.BlockSpec((d, cols), lambda i: (0, 0))
    u_spec = pl.BlockSpec((tm, cols), lambda i: (i, 0))
    vec_spec = pl.BlockSpec((1, d), lambda i: (0, 0))
    if ln is None:
        u = pl.pallas_call(
            _in_proj_kernel, grid=grid, in_specs=[x_spec, w_spec], out_specs=u_spec,
            out_shape=jax.ShapeDtypeStruct((t, cols), F32), compiler_params=_params(1),
            name="in_proj")(x2d, w_in_bf)
        return u, x2d
    g, b = ln
    u, xn = pl.pallas_call(
        _in_proj_ln_kernel, grid=grid, in_specs=[x_spec, vec_spec, vec_spec, w_spec],
        out_specs=[u_spec, x_spec],
        out_shape=[jax.ShapeDtypeStruct((t, cols), F32), jax.ShapeDtypeStruct((t, d), F32)],
        compiler_params=_params(1), name="in_proj_ln")(x2d, g.reshape(1, d), b.reshape(1, d), w_in_bf)
    return u, xn


def _lru_kernel(cur_ref, prev_ref, next_ref, cw_ref, cb_ref, wg_ref, bg_ref, lam_ref, h_ref,
                a_scr, u_scr, cin_scr, c_scr, *, reverse, tc):
    t = pl.program_id(1)
    nt = pl.num_programs(1)
    tt = (nt - 1 - t) if reverse else t
    w = LRU_WIDTH
    ng = tc // SUBLANES

    @pl.when(t == 0)
    def _():
        c_scr[...] = jnp.zeros_like(c_scr)

    cur = cur_ref[...].reshape(ng, SUBLANES, w)
    prev = jnp.where(tt > 0, prev_ref[...], 0.0)
    nxt = jnp.where(tt < nt - 1, next_ref[...], 0.0)
    rm = lax.broadcasted_iota(jnp.int32, (ng, SUBLANES, w), 1)

    def shift_down(k):
        s = pltpu.roll(cur, k, axis=1)
        before = jnp.concatenate([pltpu.roll(prev, k, axis=0)[None], s[:ng - 1]], axis=0)
        return jnp.where(rm < k, before, s)

    s_up = pltpu.roll(cur, SUBLANES - 1, axis=1)
    after = jnp.concatenate([s_up[1:], pltpu.roll(nxt, SUBLANES - 1, axis=0)[None]], axis=0)
    x_p1 = jnp.where(rm == SUBLANES - 1, after, s_up)

    cw = cw_ref[...]
    xc = shift_down(2) * cw[0:1] + shift_down(1) * cw[1:2]
    xc = xc + cur * cw[2:3]
    xc = xc + x_p1 * cw[3:4]
    xc = (xc + cb_ref[...]).reshape(tc, w)

    z = jnp.dot(xc.astype(BF16), wg_ref[...], preferred_element_type=F32) + bg_ref[...]
    r = _sigmoid(z[:, :w])
    ig = _sigmoid(z[:, w:])
    lam = lam_ref[...]
    e = jnp.exp(-jnp.abs(lam))
    e1 = 1.0 + e
    d = e1 - 1.0
    log1p_e = jnp.where(d == 0.0, e, jnp.log(e1) * (e / jnp.where(d == 0.0, 1.0, d)))
    a = jnp.exp2(((LRU_C * LOG2_E) * (jnp.minimum(lam, 0.0) - log1p_e)) * r)
    z1 = 1.0 - a * a
    u = (z1 * lax.rsqrt(jnp.maximum(z1, F32_TINY))) * (ig * xc)

    a = a.reshape(ng, SUBLANES, w)
    u = u.reshape(ng, SUBLANES, w)
    for k in (1, 2, 4):
        if reverse:
            keep = rm < SUBLANES - k
            shift = SUBLANES - k
        else:
            keep = rm >= k
            shift = k
        a_s = jnp.where(keep, pltpu.roll(a, shift, axis=1), 1.0)
        u_s = jnp.where(keep, pltpu.roll(u, shift, axis=1), 0.0)
        u = u + a * u_s
        a = a * a_s
    a_scr[...] = a.reshape(tc, w)
    u_scr[...] = u.reshape(tc, w)

    edge = 0 if reverse else SUBLANES - 1

    def chain(i, c):
        g = (ng - 1 - i) if reverse else i
        cin_scr[pl.ds(g, 1), :] = c
        r = g * SUBLANES + edge
        return u_scr[pl.ds(r, 1), :] + a_scr[pl.ds(r, 1), :] * c

    c_scr[...] = lax.fori_loop(0, ng, chain, c_scr[...], unroll=8)

    def apply(g, carry):
        r0 = pl.multiple_of(g * SUBLANES, SUBLANES)
        h_ref[pl.ds(r0, SUBLANES), :] = (u_scr[pl.ds(r0, SUBLANES), :]
                                         + a_scr[pl.ds(r0, SUBLANES), :] * cin_scr[pl.ds(g, 1), :])
        return carry

    lax.fori_loop(0, ng, apply, 0, unroll=8)


def _lru_scan(u3, conv_w, conv_b, w_gates_bf, b_gates, lam, reverse):
    bsz, n, _ = u3.shape
    w = LRU_WIDTH
    tc = min(LRU_TIME_TILE, n)
    nt = n // tc
    nb8 = n // SUBLANES
    per = tc // SUBLANES

    def tmap(t):
        return (nt - 1 - t) if reverse else t

    in_specs = [
        pl.BlockSpec((None, tc, w), lambda b, t: (b, tmap(t), COL_LRU_X)),
        pl.BlockSpec((None, SUBLANES, w), lambda b, t: (b, jnp.maximum(tmap(t) * per - 1, 0), COL_LRU_X)),
        pl.BlockSpec((None, SUBLANES, w), lambda b, t: (b, jnp.minimum((tmap(t) + 1) * per, nb8 - 1), COL_LRU_X)),
        pl.BlockSpec((CONV_W, w), lambda b, t: (0, 0)),
        pl.BlockSpec((1, w), lambda b, t: (0, 0)),
        pl.BlockSpec((w, 2 * w), lambda b, t: (0, 0)),
        pl.BlockSpec((1, 2 * w), lambda b, t: (0, 0)),
        pl.BlockSpec((1, w), lambda b, t: (0, 0)),
    ]
    out_spec = pl.BlockSpec((None, tc, w), lambda b, t: (b, tmap(t), 0))
    return pl.pallas_call(
        functools.partial(_lru_kernel, reverse=reverse, tc=tc),
        grid=(bsz, nt), in_specs=in_specs, out_specs=out_spec,
        out_shape=jax.ShapeDtypeStruct((bsz, n, w), F32),
        scratch_shapes=[pltpu.VMEM((tc, w), F32), pltpu.VMEM((tc, w), F32), pltpu.VMEM((tc // SUBLANES, w), F32),
                        pltpu.VMEM((1, w), F32)],
        compiler_params=_params(2), name="lru_bwd" if reverse else "lru_fwd",
    )(u3, u3, u3, conv_w, conv_b.reshape(1, w), w_gates_bf, b_gates.reshape(1, 2 * w), lam.reshape(1, w))


def _hgrn_kernel(q_ref, v_ref, f_ref, lb_ref, o_ref, st_scr, *, reverse, layer, tb, c):
    t = pl.program_id(1)

    @pl.when(t == 0)
    def _():
        st_scr[...] = jnp.zeros_like(st_scr)

    z = lb_ref[...]
    e = jnp.exp(z - jnp.max(z, axis=0, keepdims=True))
    p = e / jnp.sum(e, axis=0, keepdims=True)
    lb_all = jnp.sum(p[0:layer + 1], axis=0, keepdims=True) - p[0:1]

    ri = lax.broadcasted_iota(jnp.int32, (c, c), 0)
    ci = lax.broadcasted_iota(jnp.int32, (c, c), 1)
    valid = (ci >= ri) if reverse else (ci <= ri)
    tri = jnp.where(valid, 1.0, 0.0).astype(BF16)
    half = c // 2
    scale = HG_DK ** -0.5
    nchunks = tb // c
    nt_dims = (((1,), (1,)), ((), ()))
    tn_dims = (((0,), (0,)), ((), ()))

    order = [(j, h) for j in (range(nchunks - 1, -1, -1) if reverse else range(nchunks)) for h in range(HG_HEADS)]
    q_all, k_all, v_all, cum_all = {}, {}, {}, {}
    for j, h in order:
        sl = slice(j * c, (j + 1) * c)
        hl = slice(h * HG_DK, (h + 1) * HG_DK)
        lb = lb_all[:, hl]
        q_all[j, h] = _silu(q_ref[sl, hl]) * scale
        v_all[j, h] = v_ref[sl, hl].astype(BF16)
        f = lb + (1.0 - lb) * _sigmoid(f_ref[sl, hl])
        k_all[j, h] = 1.0 - f
        g = jnp.log2(f)
        g_hi = g.astype(BF16)
        r1 = g - g_hi.astype(F32)
        g_mid = r1.astype(BF16)
        g_lo = (r1 - g_mid.astype(F32)).astype(BF16)
        cum3 = jnp.dot(tri, jnp.concatenate([g_hi, g_mid, g_lo], axis=1), preferred_element_type=F32)
        cum_all[j, h] = (cum3[:, 0:HG_DK] + cum3[:, HG_DK:2 * HG_DK]) + cum3[:, 2 * HG_DK:3 * HG_DK]

    scores_all, qs_all, ds_all, dec_all = {}, {}, {}, {}
    for j, h in order:
        q, k, cum = q_all[j, h], k_all[j, h], cum_all[j, h]
        if reverse:
            mid = cum[half:half + 1]
            tot = cum[0:1]
        else:
            mid = cum[half - 1:half]
            tot = cum[c - 1:c]
        qt = q * jnp.exp2(jnp.minimum(cum - mid, HG_EXP2_CLAMP))
        kt = k * jnp.exp2(jnp.minimum(mid - cum, HG_EXP2_CLAMP))
        scores = lax.dot_general(qt.astype(BF16), kt.astype(BF16), nt_dims, preferred_element_type=F32)
        scores_all[j, h] = jnp.where(valid, scores, 0.0).astype(BF16)
        qs_all[j, h] = (q * jnp.exp2(cum)).astype(BF16)
        ks = (k * jnp.exp2(tot - cum)).astype(BF16)
        ds_all[j, h] = lax.dot_general(v_all[j, h], ks, tn_dims, preferred_element_type=F32)
        dec_all[j, h] = jnp.exp2(tot)

    intra_all = {}
    for j, h in order:
        intra_all[j, h] = jnp.dot(scores_all[j, h], v_all[j, h], preferred_element_type=F32)

    states = [st_scr[h] for h in range(HG_HEADS)]
    for j, h in order:
        st = states[h]
        o = intra_all[j, h] + lax.dot_general(qs_all[j, h], st.astype(BF16), nt_dims, preferred_element_type=F32)
        o_ref[j * c:(j + 1) * c, h * HG_DV:(h + 1) * HG_DV] = o
        states[h] = st * dec_all[j, h] + ds_all[j, h]
    for h in range(HG_HEADS):
        st_scr[h] = states[h]


def _hgrn_scan(u3, lb_raw, layer, reverse):
    bsz, n, _ = u3.shape
    depth = lb_raw.shape[0]
    tb = min(HG_TIME_TILE, n)
    nt = n // tb
    c = min(HG_CHUNK, tb)
    assert n % tb == 0 and tb % c == 0
    f_col = COL_F_BWD if reverse else COL_F_FWD

    def tmap(t):
        return (nt - 1 - t) if reverse else t

    in_specs = [
        pl.BlockSpec((None, tb, HG_W), lambda b, t: (b, tmap(t), COL_Q)),
        pl.BlockSpec((None, tb, HG_W), lambda b, t: (b, tmap(t), COL_I)),
        pl.BlockSpec((None, tb, HG_W), lambda b, t: (b, tmap(t), f_col)),
        pl.BlockSpec((depth, HG_W), lambda b, t: (0, 0)),
    ]
    out_spec = pl.BlockSpec((None, tb, HG_W), lambda b, t: (b, tmap(t), 0))
    return pl.pallas_call(
        functools.partial(_hgrn_kernel, reverse=reverse, layer=layer, tb=tb, c=c),
        grid=(bsz, nt), in_specs=in_specs, out_specs=out_spec,
        out_shape=jax.ShapeDtypeStruct((bsz, n, HG_W), F32),
        scratch_shapes=[pltpu.VMEM((HG_HEADS, HG_DV, HG_DK), F32)],
        compiler_params=_params(2), name="hgrn_bwd" if reverse else "hgrn_fwd",
    )(u3, u3, u3, lb_raw)


def _route(logits_t, b_router):
    mx = jnp.max(logits_t, axis=0, keepdims=True)
    ex = jnp.exp(logits_t - mx)
    probs = ex / jnp.sum(ex, axis=0, keepdims=True)
    sel = probs + b_router
    rows = [sel[i:i + 1] for i in range(N_EXPERTS)]
    prow = [probs[i:i + 1] for i in range(N_EXPERTS)]
    gscore = []
    for gi in range(N_GROUPS):
        m = rows[gi * EXPERTS_PER_GROUP:(gi + 1) * EXPERTS_PER_GROUP]
        best = None
        for a in range(EXPERTS_PER_GROUP):
            for b in range(a + 1, EXPERTS_PER_GROUP):
                s = m[a] + m[b]
                best = s if best is None else jnp.maximum(best, s)
        gscore.append(best)
    gbest = jnp.zeros_like(gscore[0], dtype=jnp.int32)
    gmax = gscore[0]
    for gi in range(1, N_GROUPS):
        better = gscore[gi] > gmax
        gbest = jnp.where(better, gi, gbest)
        gmax = jnp.where(better, gscore[gi], gmax)
    ms, mp = [], []
    for j in range(EXPERTS_PER_GROUP):
        s = rows[j]
        pr = prow[j]
        for gi in range(1, N_GROUPS):
            pick = gbest == gi
            s = jnp.where(pick, rows[gi * EXPERTS_PER_GROUP + j], s)
            pr = jnp.where(pick, prow[gi * EXPERTS_PER_GROUP + j], pr)
        ms.append(s)
        mp.append(pr)
    i1 = jnp.zeros_like(gbest)
    v1 = ms[0]
    for j in range(1, EXPERTS_PER_GROUP):
        better = ms[j] > v1
        i1 = jnp.where(better, j, i1)
        v1 = jnp.where(better, ms[j], v1)
    i2 = jnp.full_like(gbest, -1)
    v2 = jnp.full_like(v1, -jnp.inf)
    for j in range(EXPERTS_PER_GROUP):
        better = (i1 != j) & ((ms[j] > v2) | (i2 < 0))
        i2 = jnp.where(better, j, i2)
        v2 = jnp.where(better, ms[j], v2)
    p1 = mp[0]
    p2 = mp[0]
    for j in range(1, EXPERTS_PER_GROUP):
        p1 = jnp.where(i1 == j, mp[j], p1)
        p2 = jnp.where(i2 == j, mp[j], p2)
    den = p1 + p2
    w1 = p1 / den
    w2 = p2 / den
    first_low = i1 < i2
    lo = jnp.where(first_low, i1, i2)
    hi = jnp.where(first_low, i2, i1)
    pair = jnp.where(lo == 0, hi - 1, jnp.where(lo == 1, hi + 1, 5))
    cls = gbest * N_PAIRS + pair
    return cls, jnp.where(first_low, w1, w2), jnp.where(first_low, w2, w1)


def _mix_kernel(x_ref, hf_ref, hb_ref, ug_ref, of_ref, ob_ref, uhg_ref, ng_ref, wo_ref, g1_ref, b1_ref,
                wr_ref, br_ref, x1w_ref, cls_ref, rank_ref, cnt_ref, carry_scr, *, alpha):
    tm = x_ref.shape[0]

    @pl.when(pl.program_id(0) == 0)
    def _():
        carry_scr[...] = jnp.zeros_like(carry_scr)

    lru = (hf_ref[...] + hb_ref[...]) * jax.nn.gelu(ug_ref[...], approximate=True)
    o = of_ref[...] + ob_ref[...]
    heads = []
    for h in range(HG_HEADS):
        oh = o[:, h * HG_DV:(h + 1) * HG_DV]
        heads.append(oh * lax.rsqrt(jnp.mean(oh * oh, axis=-1, keepdims=True) + RMS_EPS))
    hg = jnp.concatenate(heads, axis=1) * ng_ref[...] * _silu(uhg_ref[...])
    mix_in = jnp.concatenate([lru, hg], axis=1).astype(BF16)
    mix = jnp.dot(mix_in, wo_ref[...], preferred_element_type=F32)
    x1 = _layer_norm(alpha * x_ref[...] + mix, g1_ref[...], b1_ref[...])
    def split(v):
        hi = v.astype(BF16)
        return hi, (v - hi.astype(F32)).astype(BF16)

    def dot_nt(a, b):
        return lax.dot_general(a, b, (((1,), (1,)), ((), ())), preferred_element_type=F32)

    wr_hi, wr_lo = split(wr_ref[...])
    x_hi, x_lo = split(x1)
    logits_t = dot_nt(wr_hi, x_hi) + (dot_nt(wr_hi, x_lo) + dot_nt(wr_lo, x_hi))
    cls, w_lo, w_hi = _route(logits_t, br_ref[...])
    rid = lax.broadcasted_iota(jnp.int32, (LANES, tm), 0)
    route_t = jnp.where(rid == 0, w_lo, jnp.where(rid == 1, w_hi, 0.0))
    x1w_ref[...] = jnp.concatenate([x1, route_t.T], axis=1)
    cid = lax.broadcasted_iota(jnp.int32, (CLASS_ROWS, tm), 0)
    onehot = jnp.where(cid == cls, 1.0, 0.0)
    si = lax.broadcasted_iota(jnp.int32, (tm, tm), 0)
    ti = lax.broadcasted_iota(jnp.int32, (tm, tm), 1)
    upper = jnp.where(si <= ti, 1.0, 0.0).astype(BF16)
    prefix = jnp.dot(onehot.astype(BF16), upper, preferred_element_type=F32)
    carry = carry_scr[...]
    rank = jnp.sum(onehot * (prefix - 1.0 + carry), axis=0, keepdims=True)
    cls_ref[...] = cls
    rank_ref[...] = rank.astype(jnp.int32)
    carry = carry + prefix[:, tm - 1:tm]
    carry_scr[...] = carry
    cnt_ref[...] = jnp.broadcast_to(carry, cnt_ref.shape)


def _mix(x2d, hf, hb, u2d, of, ob, norm_g, w_out_bf, ln_g, ln_b, w_router_t, b_router, alpha):
    t, d = x2d.shape
    w = LRU_WIDTH
    tm = min(TOKEN_TILE, t)
    assert t % tm == 0
    nt = t // tm
    row = lambda i: (i, 0)
    const = lambda i: (0, 0)
    in_specs = [
        pl.BlockSpec((tm, d), row),
        pl.BlockSpec((tm, w), row),
        pl.BlockSpec((tm, w), row),
        pl.BlockSpec((tm, w), lambda i: (i, COL_LRU_GATE)),
        pl.BlockSpec((tm, w), row),
        pl.BlockSpec((tm, w), row),
        pl.BlockSpec((tm, w), lambda i: (i, COL_G)),
        pl.BlockSpec((1, w), const),
        pl.BlockSpec((2 * w, d), const),
        pl.BlockSpec((1, d), const),
        pl.BlockSpec((1, d), const),
        pl.BlockSpec((N_EXPERTS, d), const),
        pl.BlockSpec((N_EXPERTS, 1), const),
    ]
    tok_spec = pl.BlockSpec((None, 1, tm), lambda i: (i, 0, 0))
    out_specs = [pl.BlockSpec((tm, d + LANES), row), tok_spec, tok_spec, pl.BlockSpec((CLASS_ROWS, LANES), const)]
    x1w, cls, rank, cnt = pl.pallas_call(
        functools.partial(_mix_kernel, alpha=alpha),
        grid=(nt,), in_specs=in_specs, out_specs=out_specs,
        out_shape=[jax.ShapeDtypeStruct((t, d + LANES), F32), jax.ShapeDtypeStruct((nt, 1, tm), jnp.int32),
                   jax.ShapeDtypeStruct((nt, 1, tm), jnp.int32), jax.ShapeDtypeStruct((CLASS_ROWS, LANES), F32)],
        scratch_shapes=[pltpu.VMEM((CLASS_ROWS, 1), F32)],
        compiler_params=_params(1), name="mix_ln_route",
    )(x2d, hf, hb, u2d, of, ob, u2d, norm_g.reshape(1, w), w_out_bf, ln_g.reshape(1, d), ln_b.reshape(1, d),
      w_router_t, b_router.reshape(N_EXPERTS, 1))
    return x1w, cls.reshape(t), rank.reshape(t), cnt[:N_CLASSES, 0].astype(jnp.int32)


def _route_plan(cls, rank, counts, t, tm):
    nt = t // tm + N_CLASSES
    tiles_c = (counts + tm - 1) // tm
    tile_end = jnp.cumsum(tiles_c)
    pos = ((tile_end - tiles_c) * tm)[cls] + rank
    src = jnp.zeros((nt * tm,), jnp.int32).at[pos].set(jnp.arange(t, dtype=jnp.int32))
    tile = jnp.minimum(jnp.arange(nt, dtype=jnp.int32), tile_end[-1] - 1)
    tile_cls = jnp.sum((tile_end[None, :] <= tile[:, None]).astype(jnp.int32), axis=1)
    group = tile_cls // N_PAIRS
    pair = tile_cls % N_PAIRS
    ea = group * EXPERTS_PER_GROUP + jnp.array(PAIR_LO, jnp.int32)[pair]
    eb = group * EXPERTS_PER_GROUP + jnp.array(PAIR_HI, jnp.int32)[pair]
    return pos, src.reshape(nt, 1, tm), ea, eb


def _start_row_gather(idx_ref, src_hbm, buf, sem, slot, rows):
    for r in range(rows):
        pltpu.make_async_copy(src_hbm.at[pl.ds(idx_ref[0, r], 1)], buf.at[slot, pl.ds(r, 1)], sem.at[slot]).start()


def _wait_row_gather(src_hbm, buf, sem, slot, rows):
    pltpu.make_async_copy(src_hbm.at[pl.ds(0, rows)], buf.at[slot], sem.at[slot]).wait()


def _moe_kernel(ea_ref, eb_ref, src_ref, srcn_ref, x_hbm, wga_ref, wua_ref, wda_ref, wgb_ref, wub_ref,
                wdb_ref, y_ref, xbuf, sem, *, tm, d):
    del ea_ref, eb_ref
    i = pl.program_id(0)
    nt = pl.num_programs(0)
    slot = lax.rem(i, 2)

    @pl.when(i == 0)
    def _():
        _start_row_gather(src_ref, x_hbm, xbuf, sem, 0, tm)

    _wait_row_gather(x_hbm, xbuf, sem, slot, tm)
    _start_row_gather(srcn_ref, x_hbm, xbuf, sem, 1 - slot, tm)

    xw = xbuf[slot]
    x = xw[:, :d].astype(BF16)
    w_lo = xw[:, d:d + 1]
    w_hi = xw[:, d + 1:d + 2]

    def expert(wg_ref, wu_ref, wd_ref):
        hid = _silu(jnp.dot(x, wg_ref[...], preferred_element_type=F32)) * jnp.dot(
            x, wu_ref[...], preferred_element_type=F32)
        return jnp.dot(hid.astype(BF16), wd_ref[...], preferred_element_type=F32)

    y_ref[...] = w_lo * expert(wga_ref, wua_ref, wda_ref) + w_hi * expert(wgb_ref, wub_ref, wdb_ref)

    @pl.when(i == nt - 1)
    def _():
        _wait_row_gather(x_hbm, xbuf, sem, 1 - slot, tm)


def _moe(x1w, src, ea, eb, wg_bf, wu_bf, wd_bf):
    d = x1w.shape[1] - LANES
    _, _, dff = wg_bf.shape
    nt, _, tm = src.shape
    idx_spec = pl.BlockSpec((None, 1, tm), lambda i, ea, eb: (i, 0, 0), memory_space=pltpu.SMEM)
    idx_next_spec = pl.BlockSpec((None, 1, tm), lambda i, ea, eb: (jnp.minimum(i + 1, nt - 1), 0, 0),
                                 memory_space=pltpu.SMEM)
    up_a = pl.BlockSpec((None, d, dff), lambda i, ea, eb: (ea[i], 0, 0))
    down_a = pl.BlockSpec((None, dff, d), lambda i, ea, eb: (ea[i], 0, 0))
    up_b = pl.BlockSpec((None, d, dff), lambda i, ea, eb: (eb[i], 0, 0))
    down_b = pl.BlockSpec((None, dff, d), lambda i, ea, eb: (eb[i], 0, 0))
    grid_spec = pltpu.PrefetchScalarGridSpec(
        num_scalar_prefetch=2, grid=(nt,),
        in_specs=[idx_spec, idx_next_spec, pl.BlockSpec(memory_space=pl.ANY), up_a, up_a, down_a, up_b, up_b, down_b],
        out_specs=pl.BlockSpec((tm, d), lambda i, ea, eb: (i, 0)),
        scratch_shapes=[pltpu.VMEM((2, tm, d + LANES), F32), pltpu.SemaphoreType.DMA((2,))])
    return pl.pallas_call(
        functools.partial(_moe_kernel, tm=tm, d=d), grid_spec=grid_spec,
        out_shape=jax.ShapeDtypeStruct((nt * tm, d), F32),
        compiler_params=_params(1), name="moe_routed",
    )(ea, eb, src, src, x1w, wg_bf, wu_bf, wd_bf, wg_bf, wu_bf, wd_bf)


def _ln2_gathered(pos_ref, posn_ref, x1_ref, ys_hbm, g_ref, b_ref, ybuf, sem, alpha):
    i = pl.program_id(0)
    tm = x1_ref.shape[0]
    slot = lax.rem(i, 2)

    @pl.when(i == 0)
    def _():
        _start_row_gather(pos_ref, ys_hbm, ybuf, sem, 0, tm)

    _wait_row_gather(ys_hbm, ybuf, sem, slot, tm)
    _start_row_gather(posn_ref, ys_hbm, ybuf, sem, 1 - slot, tm)
    return _layer_norm(alpha * x1_ref[...] + ybuf[slot], g_ref[...], b_ref[...])


def _ln2_finish(ys_hbm, ybuf, sem):
    i = pl.program_id(0)

    @pl.when(i == pl.num_programs(0) - 1)
    def _():
        _wait_row_gather(ys_hbm, ybuf, sem, 1 - lax.rem(i, 2), ybuf.shape[1])


def _ln2_in_proj_kernel(pos_ref, posn_ref, x1_ref, ys_hbm, g_ref, b_ref, w_ref, u_ref, x2_ref, ybuf, sem, *, alpha):
    x2 = _ln2_gathered(pos_ref, posn_ref, x1_ref, ys_hbm, g_ref, b_ref, ybuf, sem, alpha)
    x2_ref[...] = x2
    u_ref[...] = jnp.dot(x2.astype(BF16), w_ref[...], preferred_element_type=F32)
    _ln2_finish(ys_hbm, ybuf, sem)


def _ln2_kernel(pos_ref, posn_ref, x1_ref, ys_hbm, g_ref, b_ref, x2_ref, ybuf, sem, *, alpha):
    x2_ref[...] = _ln2_gathered(pos_ref, posn_ref, x1_ref, ys_hbm, g_ref, b_ref, ybuf, sem, alpha)
    _ln2_finish(ys_hbm, ybuf, sem)


def _ln2(x1w, ys, pos, ln_g, ln_b, alpha, w_in_bf=None):
    t = x1w.shape[0]
    d = ys.shape[1]
    nt, _, tm = pos.shape
    row = lambda i: (i, 0)
    const = lambda i: (0, 0)
    x_spec = pl.BlockSpec((tm, d), row)
    vec_spec = pl.BlockSpec((1, d), const)
    in_specs = [pl.BlockSpec((None, 1, tm), lambda i: (i, 0, 0), memory_space=pltpu.SMEM),
                pl.BlockSpec((None, 1, tm), lambda i: (jnp.minimum(i + 1, nt - 1), 0, 0), memory_space=pltpu.SMEM),
                x_spec, pl.BlockSpec(memory_space=pl.ANY), vec_spec, vec_spec]
    scratch = [pltpu.VMEM((2, tm, d), F32), pltpu.SemaphoreType.DMA((2,))]
    args = (pos, pos, x1w, ys, ln_g.reshape(1, d), ln_b.reshape(1, d))
    if w_in_bf is None:
        return pl.pallas_call(
            functools.partial(_ln2_kernel, alpha=alpha), grid=(nt,), in_specs=in_specs, out_specs=x_spec,
            out_shape=jax.ShapeDtypeStruct((t, d), F32), scratch_shapes=scratch,
            compiler_params=_params(1), name="ln2")(*args)
    cols = w_in_bf.shape[1]
    return pl.pallas_call(
        functools.partial(_ln2_in_proj_kernel, alpha=alpha), grid=(nt,),
        in_specs=in_specs + [pl.BlockSpec((d, cols), const)],
        out_specs=[pl.BlockSpec((tm, cols), row), x_spec],
        out_shape=[jax.ShapeDtypeStruct((t, cols), F32), jax.ShapeDtypeStruct((t, d), F32)],
        scratch_shapes=scratch, compiler_params=_params(1), name="ln2_in_proj")(*args, w_in_bf)


def _block_diag(blocks):
    nb, bw, _ = blocks.shape
    eye = jnp.eye(nb, dtype=blocks.dtype)
    return jnp.einsum("hij,hg->higj", blocks, eye).reshape(nb * bw, nb * bw)


def _trunk(x, p):
    bsz, n, d = x.shape
    depth = p["w_in"].shape[0]
    alpha = (2 * depth) ** 0.25
    t = bsz * n
    u2d, x2d = _in_proj(x.reshape(t, d), p["w_in_bf"][0], (p["ln_in_g"], p["ln_in_b"]))
    for l in range(depth):
        u3 = u2d.reshape(bsz, n, -1)
        hs, os_ = [], []
        for di, rev in enumerate((False, True)):
            hs.append(_lru_scan(u3, p["conv_w"][l], p["conv_b"][l], p["w_gates_bf"][l][di], p["b_gates"][l][di],
                                p["lru_lambda"][l][di], rev))
            os_.append(_hgrn_scan(u3, p["hgrn_lb"][di], l, rev))
        x1w, cls, rank, counts = _mix(
            x2d, hs[0].reshape(t, -1), hs[1].reshape(t, -1), u2d, os_[0].reshape(t, -1), os_[1].reshape(t, -1),
            p["hgrn_norm_g"][l], p["w_out_bf"][l], p["ln1_g"][l], p["ln1_b"][l], p["w_router_t"], p["b_router"], alpha)
        pos, src, ea, eb = _route_plan(cls, rank, counts, t, min(MOE_ROW_TILE, t))
        ys = _moe(x1w, src, ea, eb, p["w_gate_bf"][l], p["w_up_bf"][l], p["w_down_bf"][l])
        tm = min(TOKEN_TILE, t)
        pos = pos.reshape(t // tm, 1, tm)
        if l + 1 < depth:
            u2d, x2d = _ln2(x1w, ys, pos, p["ln2_g"][l], p["ln2_b"][l], alpha, p["w_in_bf"][l + 1])
        else:
            x2d = _ln2(x1w, ys, pos, p["ln2_g"][l], p["ln2_b"][l], alpha)
    return x2d.reshape(bsz, n, d)


def kernel(x_prompt, x_sample, ln_in_g, ln_in_b, w_in, conv_w, conv_b, lru_wa, lru_ba, lru_wx, lru_bx, lru_lambda,
           hgrn_lb, hgrn_norm_g, w_out, ln1_g, ln1_b, w_router, b_router, w_gate, w_up, w_down, ln2_g, ln2_b):
    depth = w_in.shape[0]
    d = w_in.shape[1]
    w_gates = jnp.stack([
        jnp.stack([jnp.concatenate([_block_diag(lru_wa[l, di]), _block_diag(lru_wx[l, di])], axis=1)
                   for di in range(2)]) for l in range(depth)])
    b_gates = jnp.concatenate([lru_ba, lru_bx], axis=-1)
    w_router_t = w_router.T
    p = dict(
        ln_in_g=ln_in_g, ln_in_b=ln_in_b, w_in_bf=w_in.astype(BF16), conv_w=conv_w, conv_b=conv_b,
        w_gates_bf=w_gates.astype(BF16), b_gates=b_gates, lru_lambda=lru_lambda, hgrn_lb=hgrn_lb,
        hgrn_norm_g=hgrn_norm_g, w_out_bf=w_out.astype(BF16), ln1_g=ln1_g, ln1_b=ln1_b,
        w_router_t=w_router_t, b_router=b_router, w_gate_bf=w_gate.astype(BF16), w_up_bf=w_up.astype(BF16),
        w_down_bf=w_down.astype(BF16), ln2_g=ln2_g, ln2_b=ln2_b, w_in=w_in)
    return (_trunk(x_prompt, p), _trunk(x_sample, p))
```

```python
import functools

import jax
import jax.numpy as jnp
from jax import lax
from jax.experimental import pallas as pl
from jax.experimental.pallas import tpu as pltpu

F32 = jnp.float32
BF16 = jnp.bfloat16

LRU_WIDTH = 512
LRU_BLOCKS = 8
CONV_W = 4
CONV_LEFT = 2
LRU_C = 8.0
HG_HEADS = 4
HG_DK = 128
HG_DV = 128
HG_W = HG_HEADS * HG_DK
N_EXPERTS = 16
N_GROUPS = 4
EXPERTS_PER_GROUP = N_EXPERTS // N_GROUPS
PAIR_LO = (0, 0, 0, 1, 1, 2)
PAIR_HI = (1, 2, 3, 2, 3, 3)
N_PAIRS = len(PAIR_LO)
N_CLASSES = N_GROUPS * N_PAIRS
CLASS_ROWS = 32
LN_EPS = 1e-5
RMS_EPS = 1e-6
F32_TINY = 1.1754944e-38
LOG2_E = 1.4426950408889634
COL_LRU_X, COL_LRU_GATE, COL_Q, COL_I, COL_F_FWD, COL_F_BWD, COL_G = range(7)

LANES = 128
SUBLANES = 8
VMEM_LIMIT_BYTES = 56 * 1024 * 1024

TOKEN_TILE = 512
LRU_TIME_TILE = 512
HG_TIME_TILE = 512
HG_CHUNK = 64
MOE_ROW_TILE = 256
HG_EXP2_CLAMP = 115.0


def _params(n_axes):
    return pltpu.CompilerParams(dimension_semantics=("arbitrary",) * n_axes,
                                vmem_limit_bytes=VMEM_LIMIT_BYTES)


def _layer_norm(x, g, b):
    mu = jnp.mean(x, axis=-1, keepdims=True)
    xc = x - mu
    var = jnp.mean(xc * xc, axis=-1, keepdims=True)
    return xc * lax.rsqrt(var + LN_EPS) * g + b


def _sigmoid(x):
    return 0.5 + 0.5 * jnp.tanh(0.5 * x)


def _silu(x):
    return x * _sigmoid(x)


def _in_proj_ln_kernel(x_ref, g_ref, b_ref, w_ref, u_ref, xn_ref):
    xn = _layer_norm(x_ref[...], g_ref[...], b_ref[...])
    xn_ref[...] = xn
    u_ref[...] = jnp.dot(xn.astype(BF16), w_ref[...], preferred_element_type=F32)


def _in_proj_kernel(x_ref, w_ref, u_ref):
    u_ref[...] = jnp.dot(x_ref[...].astype(BF16), w_ref[...], preferred_element_type=F32)


def _in_proj(x2d, w_in_bf, ln=None):
    t, d = x2d.shape
    cols = w_in_bf.shape[1]
    tm = min(TOKEN_TILE, t)
    grid = (t // tm,)
    x_spec = pl.BlockSpec((tm, d), lambda i: (i, 0))
    w_spec = pl.BlockSpec((d, cols), lambda i: (0, 0))
    u_spec = pl.BlockSpec((tm, cols), lambda i: (i, 0))
    vec_spec = pl.BlockSpec((1, d), lambda i: (0, 0))
    if ln is None:
        u = pl.pallas_call(
            _in_proj_kernel, grid=grid, in_specs=[x_spec, w_spec], out_specs=u_spec,
            out_shape=jax.ShapeDtypeStruct((t, cols), F32), compiler_params=_params(1),
            name="in_proj")(x2d, w_in_bf)
        return u, x2d
    g, b = ln
    u, xn = pl.pallas_call(
        _in_proj_ln_kernel, grid=grid, in_specs=[x_spec, vec_spec, vec_spec, w_spec],
        out_specs=[u_spec, x_spec],
        out_shape=[jax.ShapeDtypeStruct((t, cols), F32), jax.ShapeDtypeStruct((t, d), F32)],
        compiler_params=_params(1), name="in_proj_ln")(x2d, g.reshape(1, d), b.reshape(1, d), w_in_bf)
    return u, xn


def _lru_kernel(cur_ref, prev_ref, next_ref, cw_ref, cb_ref, wg_ref, bg_ref, lam_ref, h_ref,
                a_scr, u_scr, cin_scr, c_scr, *, reverse, tc):
    t = pl.program_id(1)
    nt = pl.num_programs(1)
    tt = (nt - 1 - t) if reverse else t
    w = LRU_WIDTH
    ng = tc // SUBLANES

    @pl.when(t == 0)
    def _():
        c_scr[...] = jnp.zeros_like(c_scr)

    cur = cur_ref[...].reshape(ng, SUBLANES, w)
    prev = jnp.where(tt > 0, prev_ref[...], 0.0)
    nxt = jnp.where(tt < nt - 1, next_ref[...], 0.0)
    rm = lax.broadcasted_iota(jnp.int32, (ng, SUBLANES, w), 1)

    def shift_down(k):
        s = pltpu.roll(cur, k, axis=1)
        before = jnp.concatenate([pltpu.roll(prev, k, axis=0)[None], s[:ng - 1]], axis=0)
        return jnp.where(rm < k, before, s)

    s_up = pltpu.roll(cur, SUBLANES - 1, axis=1)
    after = jnp.concatenate([s_up[1:], pltpu.roll(nxt, SUBLANES - 1, axis=0)[None]], axis=0)
    x_p1 = jnp.where(rm == SUBLANES - 1, after, s_up)

    cw = cw_ref[...]
    xc = shift_down(2) * cw[0:1] + shift_down(1) * cw[1:2]
    xc = xc + cur * cw[2:3]
    xc = xc + x_p1 * cw[3:4]
    xc = (xc + cb_ref[...]).reshape(tc, w)

    z = jnp.dot(xc.astype(BF16), wg_ref[...], preferred_element_type=F32) + bg_ref[...]
    r = _sigmoid(z[:, :w])
    ig = _sigmoid(z[:, w:])
    lam = lam_ref[...]
    e = jnp.exp(-jnp.abs(lam))
    e1 = 1.0 + e
    d = e1 - 1.0
    log1p_e = jnp.where(d == 0.0, e, jnp.log(e1) * (e / jnp.where(d == 0.0, 1.0, d)))
    a = jnp.exp2(((LRU_C * LOG2_E) * (jnp.minimum(lam, 0.0) - log1p_e)) * r)
    z1 = 1.0 - a * a
    u = (z1 * lax.rsqrt(jnp.maximum(z1, F32_TINY))) * (ig * xc)

    a = a.reshape(ng, SUBLANES, w)
    u = u.reshape(ng, SUBLANES, w)
    for k in (1, 2, 4):
        if reverse:
            keep = rm < SUBLANES - k
            shift = SUBLANES - k
        else:
            keep = rm >= k
            shift = k
        a_s = jnp.where(keep, pltpu.roll(a, shift, axis=1), 1.0)
        u_s = jnp.where(keep, pltpu.roll(u, shift, axis=1), 0.0)
        u = u + a * u_s
        a = a * a_s
    a_scr[...] = a.reshape(tc, w)
    u_scr[...] = u.reshape(tc, w)

    edge = 0 if reverse else SUBLANES - 1

    def chain(i, c):
        g = (ng - 1 - i) if reverse else i
        cin_scr[pl.ds(g, 1), :] = c
        r = g * SUBLANES + edge
        return u_scr[pl.ds(r, 1), :] + a_scr[pl.ds(r, 1), :] * c

    c_scr[...] = lax.fori_loop(0, ng, chain, c_scr[...], unroll=8)

    def apply(g, carry):
        r0 = pl.multiple_of(g * SUBLANES, SUBLANES)
        h_ref[pl.ds(r0, SUBLANES), :] = (u_scr[pl.ds(r0, SUBLANES), :]
                                         + a_scr[pl.ds(r0, SUBLANES), :] * cin_scr[pl.ds(g, 1), :])
        return carry

    lax.fori_loop(0, ng, apply, 0, unroll=8)


def _lru_scan(u3, conv_w, conv_b, w_gates_bf, b_gates, lam, reverse):
    bsz, n, _ = u3.shape
    w = LRU_WIDTH
    tc = min(LRU_TIME_TILE, n)
    nt = n // tc
    nb8 = n // SUBLANES
    per = tc // SUBLANES

    def tmap(t):
        return (nt - 1 - t) if reverse else t

    in_specs = [
        pl.BlockSpec((None, tc, w), lambda b, t: (b, tmap(t), COL_LRU_X)),
        pl.BlockSpec((None, SUBLANES, w), lambda b, t: (b, jnp.maximum(tmap(t) * per - 1, 0), COL_LRU_X)),
        pl.BlockSpec((None, SUBLANES, w), lambda b, t: (b, jnp.minimum((tmap(t) + 1) * per, nb8 - 1), COL_LRU_X)),
        pl.BlockSpec((CONV_W, w), lambda b, t: (0, 0)),
        pl.BlockSpec((1, w), lambda b, t: (0, 0)),
        pl.BlockSpec((w, 2 * w), lambda b, t: (0, 0)),
        pl.BlockSpec((1, 2 * w), lambda b, t: (0, 0)),
        pl.BlockSpec((1, w), lambda b, t: (0, 0)),
    ]
    out_spec = pl.BlockSpec((None, tc, w), lambda b, t: (b, tmap(t), 0))
    return pl.pallas_call(
        functools.partial(_lru_kernel, reverse=reverse, tc=tc),
        grid=(bsz, nt), in_specs=in_specs, out_specs=out_spec,
        out_shape=jax.ShapeDtypeStruct((bsz, n, w), F32),
        scratch_shapes=[pltpu.VMEM((tc, w), F32), pltpu.VMEM((tc, w), F32), pltpu.VMEM((tc // SUBLANES, w), F32),
                        pltpu.VMEM((1, w), F32)],
        compiler_params=_params(2), name="lru_bwd" if reverse else "lru_fwd",
    )(u3, u3, u3, conv_w, conv_b.reshape(1, w), w_gates_bf, b_gates.reshape(1, 2 * w), lam.reshape(1, w))


def _hgrn_kernel(q_ref, v_ref, f_ref, lb_ref, o_ref, st_scr, *, reverse, layer, tb, c):
    t = pl.program_id(1)

    @pl.when(t == 0)
    def _():
        st_scr[...] = jnp.zeros_like(st_scr)

    z = lb_ref[...]
    e = jnp.exp(z - jnp.max(z, axis=0, keepdims=True))
    p = e / jnp.sum(e, axis=0, keepdims=True)
    lb_all = jnp.sum(p[0:layer + 1], axis=0, keepdims=True) - p[0:1]

    ri = lax.broadcasted_iota(jnp.int32, (c, c), 0)
    ci = lax.broadcasted_iota(jnp.int32, (c, c), 1)
    valid = (ci >= ri) if reverse else (ci <= ri)
    tri = jnp.where(valid, 1.0, 0.0).astype(BF16)
    half = c // 2
    scale = HG_DK ** -0.5
    nchunks = tb // c
    nt_dims = (((1,), (1,)), ((), ()))
    tn_dims = (((0,), (0,)), ((), ()))

    order = [(j, h) for j in (range(nchunks - 1, -1, -1) if reverse else range(nchunks)) for h in range(HG_HEADS)]
    q_all, k_all, v_all, cum_all = {}, {}, {}, {}
    for j, h in order:
        sl = slice(j * c, (j + 1) * c)
        hl = slice(h * HG_DK, (h + 1) * HG_DK)
        lb = lb_all[:, hl]
        q_all[j, h] = _silu(q_ref[sl, hl]) * scale
        v_all[j, h] = v_ref[sl, hl].astype(BF16)
        f = lb + (1.0 - lb) * _sigmoid(f_ref[sl, hl])
        k_all[j, h] = 1.0 - f
        g = jnp.log2(f)
        g_hi = g.astype(BF16)
        r1 = g - g_hi.astype(F32)
        g_mid = r1.astype(BF16)
        g_lo = (r1 - g_mid.astype(F32)).astype(BF16)
        cum3 = jnp.dot(tri, jnp.concatenate([g_hi, g_mid, g_lo], axis=1), preferred_element_type=F32)
        cum_all[j, h] = (cum3[:, 0:HG_DK] + cum3[:, HG_DK:2 * HG_DK]) + cum3[:, 2 * HG_DK:3 * HG_DK]

    scores_all, qs_all, ds_all, dec_all = {}, {}, {}, {}
    for j, h in order:
        q, k, cum = q_all[j, h], k_all[j, h], cum_all[j, h]
        if reverse:
            mid = cum[half:half + 1]
            tot = cum[0:1]
        else:
            mid = cum[half - 1:half]
            tot = cum[c - 1:c]
        qt = q * jnp.exp2(jnp.minimum(cum - mid, HG_EXP2_CLAMP))
        kt = k * jnp.exp2(jnp.minimum(mid - cum, HG_EXP2_CLAMP))
        scores = lax.dot_general(qt.astype(BF16), kt.astype(BF16), nt_dims, preferred_element_type=F32)
        scores_all[j, h] = jnp.where(valid, scores, 0.0).astype(BF16)
        qs_all[j, h] = (q * jnp.exp2(cum)).astype(BF16)
        ks = (k * jnp.exp2(tot - cum)).astype(BF16)
        ds_all[j, h] = lax.dot_general(v_all[j, h], ks, tn_dims, preferred_element_type=F32)
        dec_all[j, h] = jnp.exp2(tot)

    intra_all = {}
    for j, h in order:
        intra_all[j, h] = jnp.dot(scores_all[j, h], v_all[j, h], preferred_element_type=F32)

    states = [st_scr[h] for h in range(HG_HEADS)]
    for j, h in order:
        st = states[h]
        o = intra_all[j, h] + lax.dot_general(qs_all[j, h], st.astype(BF16), nt_dims, preferred_element_type=F32)
        o_ref[j * c:(j + 1) * c, h * HG_DV:(h + 1) * HG_DV] = o
        states[h] = st * dec_all[j, h] + ds_all[j, h]
    for h in range(HG_HEADS):
        st_scr[h] = states[h]


def _hgrn_scan(u3, lb_raw, layer, reverse):
    bsz, n, _ = u3.shape
    depth = lb_raw.shape[0]
    tb = min(HG_TIME_TILE, n)
    nt = n // tb
    c = min(HG_CHUNK, tb)
    assert n % tb == 0 and tb % c == 0
    f_col = COL_F_BWD if reverse else COL_F_FWD

    def tmap(t):
        return (nt - 1 - t) if reverse else t

    in_specs = [
        pl.BlockSpec((None, tb, HG_W), lambda b, t: (b, tmap(t), COL_Q)),
        pl.BlockSpec((None, tb, HG_W), lambda b, t: (b, tmap(t), COL_I)),
        pl.BlockSpec((None, tb, HG_W), lambda b, t: (b, tmap(t), f_col)),
        pl.BlockSpec((depth, HG_W), lambda b, t: (0, 0)),
    ]
    out_spec = pl.BlockSpec((None, tb, HG_W), lambda b, t: (b, tmap(t), 0))
    return pl.pallas_call(
        functools.partial(_hgrn_kernel, reverse=reverse, layer=layer, tb=tb, c=c),
        grid=(bsz, nt), in_specs=in_specs, out_specs=out_spec,
        out_shape=jax.ShapeDtypeStruct((bsz, n, HG_W), F32),
        scratch_shapes=[pltpu.VMEM((HG_HEADS, HG_DV, HG_DK), F32)],
        compiler_params=_params(2), name="hgrn_bwd" if reverse else "hgrn_fwd",
    )(u3, u3, u3, lb_raw)


def _route(logits_t, b_router):
    mx = jnp.max(logits_t, axis=0, keepdims=True)
    ex = jnp.exp(logits_t - mx)
    probs = ex / jnp.sum(ex, axis=0, keepdims=True)
    sel = probs + b_router
    rows = [sel[i:i + 1] for i in range(N_EXPERTS)]
    prow = [probs[i:i + 1] for i in range(N_EXPERTS)]
    gscore = []
    for gi in range(N_GROUPS):
        m = rows[gi * EXPERTS_PER_GROUP:(gi + 1) * EXPERTS_PER_GROUP]
        best = None
        for a in range(EXPERTS_PER_GROUP):
            for b in range(a + 1, EXPERTS_PER_GROUP):
                s = m[a] + m[b]
                best = s if best is None else jnp.maximum(best, s)
        gscore.append(best)
    gbest = jnp.zeros_like(gscore[0], dtype=jnp.int32)
    gmax = gscore[0]
    for gi in range(1, N_GROUPS):
        better = gscore[gi] > gmax
        gbest = jnp.where(better, gi, gbest)
        gmax = jnp.where(better, gscore[gi], gmax)
    ms, mp = [], []
    for j in range(EXPERTS_PER_GROUP):
        s = rows[j]
        pr = prow[j]
        for gi in range(1, N_GROUPS):
            pick = gbest == gi
            s = jnp.where(pick, rows[gi * EXPERTS_PER_GROUP + j], s)
            pr = jnp.where(pick, prow[gi * EXPERTS_PER_GROUP + j], pr)
        ms.append(s)
        mp.append(pr)
    i1 = jnp.zeros_like(gbest)
    v1 = ms[0]
    for j in range(1, EXPERTS_PER_GROUP):
        better = ms[j] > v1
        i1 = jnp.where(better, j, i1)
        v1 = jnp.where(better, ms[j], v1)
    i2 = jnp.full_like(gbest, -1)
    v2 = jnp.full_like(v1, -jnp.inf)
    for j in range(EXPERTS_PER_GROUP):
        better = (i1 != j) & ((ms[j] > v2) | (i2 < 0))
        i2 = jnp.where(better, j, i2)
        v2 = jnp.where(better, ms[j], v2)
    p1 = mp[0]
    p2 = mp[0]
    for j in range(1, EXPERTS_PER_GROUP):
        p1 = jnp.where(i1 == j, mp[j], p1)
        p2 = jnp.where(i2 == j, mp[j], p2)
    den = p1 + p2
    w1 = p1 / den
    w2 = p2 / den
    first_low = i1 < i2
    lo = jnp.where(first_low, i1, i2)
    hi = jnp.where(first_low, i2, i1)
    pair = jnp.where(lo == 0, hi - 1, jnp.where(lo == 1, hi + 1, 5))
    cls = gbest * N_PAIRS + pair
    return cls, jnp.where(first_low, w1, w2), jnp.where(first_low, w2, w1)


def _mix_kernel(x_ref, hf_ref, hb_ref, ug_ref, of_ref, ob_ref, uhg_ref, ng_ref, wo_ref, g1_ref, b1_ref,
                wr_ref, br_ref, x1w_ref, cls_ref, rank_ref, cnt_ref, carry_scr, *, alpha):
    tm = x_ref.shape[0]

    @pl.when(pl.program_id(0) == 0)
    def _():
        carry_scr[...] = jnp.zeros_like(carry_scr)

    lru = (hf_ref[...] + hb_ref[...]) * jax.nn.gelu(ug_ref[...], approximate=True)
    o = of_ref[...] + ob_ref[...]
    heads = []
    for h in range(HG_HEADS):
        oh = o[:, h * HG_DV:(h + 1) * HG_DV]
        heads.append(oh * lax.rsqrt(jnp.mean(oh * oh, axis=-1, keepdims=True) + RMS_EPS))
    hg = jnp.concatenate(heads, axis=1) * ng_ref[...] * _silu(uhg_ref[...])
    mix_in = jnp.concatenate([lru, hg], axis=1).astype(BF16)
    mix = jnp.dot(mix_in, wo_ref[...], preferred_element_type=F32)
    x1 = _layer_norm(alpha * x_ref[...] + mix, g1_ref[...], b1_ref[...])
    def split(v):
        hi = v.astype(BF16)
        return hi, (v - hi.astype(F32)).astype(BF16)

    def dot_nt(a, b):
        return lax.dot_general(a, b, (((1,), (1,)), ((), ())), preferred_element_type=F32)

    wr_hi, wr_lo = split(wr_ref[...])
    x_hi, x_lo = split(x1)
    logits_t = dot_nt(wr_hi, x_hi) + (dot_nt(wr_hi, x_lo) + dot_nt(wr_lo, x_hi))
    cls, w_lo, w_hi = _route(logits_t, br_ref[...])
    rid = lax.broadcasted_iota(jnp.int32, (LANES, tm), 0)
    route_t = jnp.where(rid == 0, w_lo, jnp.where(rid == 1, w_hi, 0.0))
    x1w_ref[...] = jnp.concatenate([x1, route_t.T], axis=1)
    cid = lax.broadcasted_iota(jnp.int32, (CLASS_ROWS, tm), 0)
    onehot = jnp.where(cid == cls, 1.0, 0.0)
    si = lax.broadcasted_iota(jnp.int32, (tm, tm), 0)
    ti = lax.broadcasted_iota(jnp.int32, (tm, tm), 1)
    upper = jnp.where(si <= ti, 1.0, 0.0).astype(BF16)
    prefix = jnp.dot(onehot.astype(BF16), upper, preferred_element_type=F32)
    carry = carry_scr[...]
    rank = jnp.sum(onehot * (prefix - 1.0 + carry), axis=0, keepdims=True)
    cls_ref[...] = cls
    rank_ref[...] = rank.astype(jnp.int32)
    carry = carry + prefix[:, tm - 1:tm]
    carry_scr[...] = carry
    cnt_ref[...] = jnp.broadcast_to(carry, cnt_ref.shape)


def _mix(x2d, hf, hb, u2d, of, ob, norm_g, w_out_bf, ln_g, ln_b, w_router_t, b_router, alpha):
    t, d = x2d.shape
    w = LRU_WIDTH
    tm = min(TOKEN_TILE, t)
    assert t % tm == 0
    nt = t // tm
    row = lambda i: (i, 0)
    const = lambda i: (0, 0)
    in_specs = [
        pl.BlockSpec((tm, d), row),
        pl.BlockSpec((tm, w), row),
        pl.BlockSpec((tm, w), row),
        pl.BlockSpec((tm, w), lambda i: (i, COL_LRU_GATE)),
        pl.BlockSpec((tm, w), row),
        pl.BlockSpec((tm, w), row),
        pl.BlockSpec((tm, w), lambda i: (i, COL_G)),
        pl.BlockSpec((1, w), const),
        pl.BlockSpec((2 * w, d), const),
        pl.BlockSpec((1, d), const),
        pl.BlockSpec((1, d), const),
        pl.BlockSpec((N_EXPERTS, d), const),
        pl.BlockSpec((N_EXPERTS, 1), const),
    ]
    tok_spec = pl.BlockSpec((None, 1, tm), lambda i: (i, 0, 0))
    out_specs = [pl.BlockSpec((tm, d + LANES), row), tok_spec, tok_spec, pl.BlockSpec((CLASS_ROWS, LANES), const)]
    x1w, cls, rank, cnt = pl.pallas_call(
        functools.partial(_mix_kernel, alpha=alpha),
        grid=(nt,), in_specs=in_specs, out_specs=out_specs,
        out_shape=[jax.ShapeDtypeStruct((t, d + LANES), F32), jax.ShapeDtypeStruct((nt, 1, tm), jnp.int32),
                   jax.ShapeDtypeStruct((nt, 1, tm), jnp.int32), jax.ShapeDtypeStruct((CLASS_ROWS, LANES), F32)],
        scratch_shapes=[pltpu.VMEM((CLASS_ROWS, 1), F32)],
        compiler_params=_params(1), name="mix_ln_route",
    )(x2d, hf, hb, u2d, of, ob, u2d, norm_g.reshape(1, w), w_out_bf, ln_g.reshape(1, d), ln_b.reshape(1, d),
      w_router_t, b_router.reshape(N_EXPERTS, 1))
    return x1w, cls.reshape(t), rank.reshape(t), cnt[:N_CLASSES, 0].astype(jnp.int32)


def _route_plan(cls, rank, counts, t, tm):
    nt = t // tm + N_CLASSES
    tiles_c = (counts + tm - 1) // tm
    tile_end = jnp.cumsum(tiles_c)
    pos = ((tile_end - tiles_c) * tm)[cls] + rank
    last_tile = jnp.where(tiles_c > 0, tile_end - 1, -1).astype(jnp.int32)
    used = tile_end[-1:].astype(jnp.int32)
    xt = jnp.minimum(jnp.arange(nt, dtype=jnp.int32), tile_end[-1] - 1)
    tile_cls = jnp.sum((tile_end[None, :] <= xt[:, None]).astype(jnp.int32), axis=1)
    group = tile_cls // N_PAIRS
    pair = tile_cls % N_PAIRS
    ea = group * EXPERTS_PER_GROUP + jnp.array(PAIR_LO, jnp.int32)[pair]
    eb = group * EXPERTS_PER_GROUP + jnp.array(PAIR_HI, jnp.int32)[pair]
    return pos, last_tile, used, xt, ea, eb


def _dispatch_kernel(last_ref, used_ref, pos_ref, x_hbm, xs_hbm, zbuf, sem_z, sem, *, td, tm, n_tiles):
    i = pl.program_id(0)

    def zero_tile(j):
        cp = pltpu.make_async_copy(zbuf, xs_hbm.at[pl.ds(pl.multiple_of(j * tm, tm), tm)], sem_z.at[0])
        cp.start()
        cp.wait()

    @pl.when(i == 0)
    def _():
        zbuf[...] = jnp.zeros_like(zbuf)
        for c in range(N_CLASSES):
            @pl.when(last_ref[c] >= 0)
            def _():
                zero_tile(last_ref[c])

        def tail(j, carry):
            zero_tile(j)
            return carry
        lax.fori_loop(used_ref[0], n_tiles, tail, 0)

    for r in range(td):
        pltpu.make_async_copy(x_hbm.at[pl.ds(i * td + r, 1)], xs_hbm.at[pl.ds(pos_ref[0, r], 1)], sem.at[0]).start()

    def wait_step():
        pltpu.make_async_copy(x_hbm.at[pl.ds(0, td)], xs_hbm.at[pl.ds(0, td)], sem.at[0]).wait()

    @pl.when(i > 0)
    def _():
        wait_step()

    @pl.when(i == pl.num_programs(0) - 1)
    def _():
        wait_step()


def _dispatch(x1w, pos, last_tile, used, tm, n_tiles):
    t, w = x1w.shape
    nt, _, td = pos.shape
    grid_spec = pltpu.PrefetchScalarGridSpec(
        num_scalar_prefetch=2, grid=(nt,),
        in_specs=[pl.BlockSpec((None, 1, td), lambda i, lt, us: (i, 0, 0), memory_space=pltpu.SMEM),
                  pl.BlockSpec(memory_space=pl.ANY)],
        out_specs=pl.BlockSpec(memory_space=pl.ANY),
        scratch_shapes=[pltpu.VMEM((tm, w), F32), pltpu.SemaphoreType.DMA((1,)), pltpu.SemaphoreType.DMA((1,))])
    return pl.pallas_call(
        functools.partial(_dispatch_kernel, td=td, tm=tm, n_tiles=n_tiles), grid_spec=grid_spec,
        out_shape=jax.ShapeDtypeStruct((n_tiles * tm, w), F32),
        compiler_params=_params(1), name="dispatch",
    )(last_tile, used, pos, x1w)


def _start_row_gather(idx_ref, src_hbm, buf, sem, slot, rows):
    for r in range(rows):
        pltpu.make_async_copy(src_hbm.at[pl.ds(idx_ref[0, r], 1)], buf.at[slot, pl.ds(r, 1)], sem.at[slot]).start()


def _wait_row_gather(src_hbm, buf, sem, slot, rows):
    pltpu.make_async_copy(src_hbm.at[pl.ds(0, rows)], buf.at[slot], sem.at[slot]).wait()


def _moe_kernel(xt_ref, ea_ref, eb_ref, x_ref, wga_ref, wua_ref, wda_ref, wgb_ref, wub_ref, wdb_ref, y_ref, *, d):
    del xt_ref, ea_ref, eb_ref
    xw = x_ref[...]
    x = xw[:, :d].astype(BF16)
    w_lo = xw[:, d:d + 1]
    w_hi = xw[:, d + 1:d + 2]

    def expert(wg_ref, wu_ref, wd_ref):
        hid = _silu(jnp.dot(x, wg_ref[...], preferred_element_type=F32)) * jnp.dot(
            x, wu_ref[...], preferred_element_type=F32)
        return jnp.dot(hid.astype(BF16), wd_ref[...], preferred_element_type=F32)

    y_ref[...] = w_lo * expert(wga_ref, wua_ref, wda_ref) + w_hi * expert(wgb_ref, wub_ref, wdb_ref)


def _moe(xs, xt, ea, eb, tm, wg_bf, wu_bf, wd_bf):
    d = xs.shape[1] - LANES
    _, _, dff = wg_bf.shape
    nt = xs.shape[0] // tm
    up_a = pl.BlockSpec((None, d, dff), lambda i, xt, ea, eb: (ea[i], 0, 0))
    down_a = pl.BlockSpec((None, dff, d), lambda i, xt, ea, eb: (ea[i], 0, 0))
    up_b = pl.BlockSpec((None, d, dff), lambda i, xt, ea, eb: (eb[i], 0, 0))
    down_b = pl.BlockSpec((None, dff, d), lambda i, xt, ea, eb: (eb[i], 0, 0))
    grid_spec = pltpu.PrefetchScalarGridSpec(
        num_scalar_prefetch=3, grid=(nt,),
        in_specs=[pl.BlockSpec((tm, d + LANES), lambda i, xt, ea, eb: (xt[i], 0)),
                  up_a, up_a, down_a, up_b, up_b, down_b],
        out_specs=pl.BlockSpec((tm, d), lambda i, xt, ea, eb: (i, 0)))
    return pl.pallas_call(
        functools.partial(_moe_kernel, d=d), grid_spec=grid_spec,
        out_shape=jax.ShapeDtypeStruct((nt * tm, d), F32),
        compiler_params=_params(1), name="moe_routed",
    )(xt, ea, eb, xs, wg_bf, wu_bf, wd_bf, wg_bf, wu_bf, wd_bf)


def _ln2_gathered(pos_ref, posn_ref, x1_ref, ys_hbm, g_ref, b_ref, ybuf, sem, alpha):
    i = pl.program_id(0)
    tm = x1_ref.shape[0]
    slot = lax.rem(i, 2)

    @pl.when(i == 0)
    def _():
        _start_row_gather(pos_ref, ys_hbm, ybuf, sem, 0, tm)

    _wait_row_gather(ys_hbm, ybuf, sem, slot, tm)
    _start_row_gather(posn_ref, ys_hbm, ybuf, sem, 1 - slot, tm)
    return _layer_norm(alpha * x1_ref[...] + ybuf[slot], g_ref[...], b_ref[...])


def _ln2_finish(ys_hbm, ybuf, sem):
    i = pl.program_id(0)

    @pl.when(i == pl.num_programs(0) - 1)
    def _():
        _wait_row_gather(ys_hbm, ybuf, sem, 1 - lax.rem(i, 2), ybuf.shape[1])


def _ln2_in_proj_kernel(pos_ref, posn_ref, x1_ref, ys_hbm, g_ref, b_ref, w_ref, u_ref, x2_ref, ybuf, sem, *, alpha):
    x2 = _ln2_gathered(pos_ref, posn_ref, x1_ref, ys_hbm, g_ref, b_ref, ybuf, sem, alpha)
    x2_ref[...] = x2
    u_ref[...] = jnp.dot(x2.astype(BF16), w_ref[...], preferred_element_type=F32)
    _ln2_finish(ys_hbm, ybuf, sem)


def _ln2_kernel(pos_ref, posn_ref, x1_ref, ys_hbm, g_ref, b_ref, x2_ref, ybuf, sem, *, alpha):
    x2_ref[...] = _ln2_gathered(pos_ref, posn_ref, x1_ref, ys_hbm, g_ref, b_ref, ybuf, sem, alpha)
    _ln2_finish(ys_hbm, ybuf, sem)


def _ln2(x1w, ys, pos, ln_g, ln_b, alpha, w_in_bf=None):
    t = x1w.shape[0]
    d = ys.shape[1]
    nt, _, tm = pos.shape
    row = lambda i: (i, 0)
    const = lambda i: (0, 0)
    x_spec = pl.BlockSpec((tm, d), row)
    vec_spec = pl.BlockSpec((1, d), const)
    in_specs = [pl.BlockSpec((None, 1, tm), lambda i: (i, 0, 0), memory_space=pltpu.SMEM),
                pl.BlockSpec((None, 1, tm), lambda i: (jnp.minimum(i + 1, nt - 1), 0, 0), memory_space=pltpu.SMEM),
                x_spec, pl.BlockSpec(memory_space=pl.ANY), vec_spec, vec_spec]
    scratch = [pltpu.VMEM((2, tm, d), F32), pltpu.SemaphoreType.DMA((2,))]
    args = (pos, pos, x1w, ys, ln_g.reshape(1, d), ln_b.reshape(1, d))
    if w_in_bf is None:
        return pl.pallas_call(
            functools.partial(_ln2_kernel, alpha=alpha), grid=(nt,), in_specs=in_specs, out_specs=x_spec,
            out_shape=jax.ShapeDtypeStruct((t, d), F32), scratch_shapes=scratch,
            compiler_params=_params(1), name="ln2")(*args)
    cols = w_in_bf.shape[1]
    return pl.pallas_call(
        functools.partial(_ln2_in_proj_kernel, alpha=alpha), grid=(nt,),
        in_specs=in_specs + [pl.BlockSpec((d, cols), const)],
        out_specs=[pl.BlockSpec((tm, cols), row), x_spec],
        out_shape=[jax.ShapeDtypeStruct((t, cols), F32), jax.ShapeDtypeStruct((t, d), F32)],
        scratch_shapes=scratch, compiler_params=_params(1), name="ln2_in_proj")(*args, w_in_bf)


def _block_diag(blocks):
    nb, bw, _ = blocks.shape
    eye = jnp.eye(nb, dtype=blocks.dtype)
    return jnp.einsum("hij,hg->higj", blocks, eye).reshape(nb * bw, nb * bw)


def _trunk(x, p):
    bsz, n, d = x.shape
    depth = p["w_in"].shape[0]
    alpha = (2 * depth) ** 0.25
    t = bsz * n
    u2d, x2d = _in_proj(x.reshape(t, d), p["w_in_bf"][0], (p["ln_in_g"], p["ln_in_b"]))
    for l in range(depth):
        u3 = u2d.reshape(bsz, n, -1)
        hs, os_ = [], []
        for di, rev in enumerate((False, True)):
            hs.append(_lru_scan(u3, p["conv_w"][l], p["conv_b"][l], p["w_gates_bf"][l][di], p["b_gates"][l][di],
                                p["lru_lambda"][l][di], rev))
            os_.append(_hgrn_scan(u3, p["hgrn_lb"][di], l, rev))
        x1w, cls, rank, counts = _mix(
            x2d, hs[0].reshape(t, -1), hs[1].reshape(t, -1), u2d, os_[0].reshape(t, -1), os_[1].reshape(t, -1),
            p["hgrn_norm_g"][l], p["w_out_bf"][l], p["ln1_g"][l], p["ln1_b"][l], p["w_router_t"], p["b_router"], alpha)
        tile_rows = min(MOE_ROW_TILE, t)
        pos, last_tile, used, xt, ea, eb = _route_plan(cls, rank, counts, t, tile_rows)
        tm = min(TOKEN_TILE, t)
        pos = pos.reshape(t // tm, 1, tm)
        xs = _dispatch(x1w, pos, last_tile, used, tile_rows, xt.shape[0])
        ys = _moe(xs, xt, ea, eb, tile_rows, p["w_gate_bf"][l], p["w_up_bf"][l], p["w_down_bf"][l])
        if l + 1 < depth:
            u2d, x2d = _ln2(x1w, ys, pos, p["ln2_g"][l], p["ln2_b"][l], alpha, p["w_in_bf"][l + 1])
        else:
            x2d = _ln2(x1w, ys, pos, p["ln2_g"][l], p["ln2_b"][l], alpha)
    return x2d.reshape(bsz, n, d)


def kernel(x_prompt, x_sample, ln_in_g, ln_in_b, w_in, conv_w, conv_b, lru_wa, lru_ba, lru_wx, lru_bx, lru_lambda,
           hgrn_lb, hgrn_norm_g, w_out, ln1_g, ln1_b, w_router, b_router, w_gate, w_up, w_down, ln2_g, ln2_b):
    depth = w_in.shape[0]
    d = w_in.shape[1]
    w_gates = jnp.stack([
        jnp.stack([jnp.concatenate([_block_diag(lru_wa[l, di]), _block_diag(lru_wx[l, di])], axis=1)
                   for di in range(2)]) for l in range(depth)])
    b_gates = jnp.concatenate([lru_ba, lru_bx], axis=-1)
    w_router_t = w_router.T
    p = dict(
        ln_in_g=ln_in_g, ln_in_b=ln_in_b, w_in_bf=w_in.astype(BF16), conv_w=conv_w, conv_b=conv_b,
        w_gates_bf=w_gates.astype(BF16), b_gates=b_gates, lru_lambda=lru_lambda, hgrn_lb=hgrn_lb,
        hgrn_norm_g=hgrn_norm_g, w_out_bf=w_out.astype(BF16), ln1_g=ln1_g, ln1_b=ln1_b,
        w_router_t=w_router_t, b_router=b_router, w_gate_bf=w_gate.astype(BF16), w_up_bf=w_up.astype(BF16),
        w_down_bf=w_down.astype(BF16), ln2_g=ln2_g, ln2_b=ln2_b, w_in=w_in)
    return (_trunk(x_prompt, p), _trunk(x_sample, p))
```

```python
import functools

import jax
import jax.numpy as jnp
from jax import lax
from jax.experimental import pallas as pl
from jax.experimental.pallas import tpu as pltpu

F32 = jnp.float32
BF16 = jnp.bfloat16

LRU_WIDTH = 512
LRU_BLOCKS = 8
CONV_W = 4
CONV_LEFT = 2
LRU_C = 8.0
HG_HEADS = 4
HG_DK = 128
HG_DV = 128
HG_W = HG_HEADS * HG_DK
N_EXPERTS = 16
N_GROUPS = 4
EXPERTS_PER_GROUP = N_EXPERTS // N_GROUPS
PAIR_LO = (0, 0, 0, 1, 1, 2)
PAIR_HI = (1, 2, 3, 2, 3, 3)
N_PAIRS = len(PAIR_LO)
N_CLASSES = N_GROUPS * N_PAIRS
CLASS_ROWS = 32
LN_EPS = 1e-5
RMS_EPS = 1e-6
F32_TINY = 1.1754944e-38
LOG2_E = 1.4426950408889634
COL_LRU_X, COL_LRU_GATE, COL_Q, COL_I, COL_F_FWD, COL_F_BWD, COL_G = range(7)

LANES = 128
SUBLANES = 8
VMEM_LIMIT_BYTES = 56 * 1024 * 1024

TOKEN_TILE = 512
LRU_TIME_TILE = 512
HG_TIME_TILE = 512
HG_CHUNK = 64
MOE_ROW_TILE = 256
HG_EXP2_CLAMP = 115.0


def _params(n_axes):
    return pltpu.CompilerParams(dimension_semantics=("arbitrary",) * n_axes,
                                vmem_limit_bytes=VMEM_LIMIT_BYTES)


def _layer_norm(x, g, b):
    mu = jnp.mean(x, axis=-1, keepdims=True)
    xc = x - mu
    var = jnp.mean(xc * xc, axis=-1, keepdims=True)
    return xc * lax.rsqrt(var + LN_EPS) * g + b


def _sigmoid(x):
    return 0.5 + 0.5 * jnp.tanh(0.5 * x)


def _silu(x):
    return x * _sigmoid(x)


def _in_proj_ln_kernel(x_ref, g_ref, b_ref, w_ref, u_ref, xn_ref):
    xn = _layer_norm(x_ref[...], g_ref[...], b_ref[...])
    xn_ref[...] = xn
    u_ref[...] = jnp.dot(xn.astype(BF16), w_ref[...], preferred_element_type=F32)


def _in_proj_kernel(x_ref, w_ref, u_ref):
    u_ref[...] = jnp.dot(x_ref[...].astype(BF16), w_ref[...], preferred_element_type=F32)


def _in_proj(x2d, w_in_bf, ln=None):
    t, d = x2d.shape
    cols = w_in_bf.shape[1]
    tm = min(TOKEN_TILE, t)
    grid = (t // tm,)
    x_spec = pl.BlockSpec((tm, d), lambda i: (i, 0))
    w_spec = pl.BlockSpec((d, cols), lambda i: (0, 0))
    u_spec = pl.BlockSpec((tm, cols), lambda i: (i, 0))
    vec_spec = pl.BlockSpec((1, d), lambda i: (0, 0))
    if ln is None:
        u = pl.pallas_call(
            _in_proj_kernel, grid=grid, in_specs=[x_spec, w_spec], out_specs=u_spec,
            out_shape=jax.ShapeDtypeStruct((t, cols), F32), compiler_params=_params(1),
            name="in_proj")(x2d, w_in_bf)
        return u, x2d
    g, b = ln
    u, xn = pl.pallas_call(
        _in_proj_ln_kernel, grid=grid, in_specs=[x_spec, vec_spec, vec_spec, w_spec],
        out_specs=[u_spec, x_spec],
        out_shape=[jax.ShapeDtypeStruct((t, cols), F32), jax.ShapeDtypeStruct((t, d), F32)],
        compiler_params=_params(1), name="in_proj_ln")(x2d, g.reshape(1, d), b.reshape(1, d), w_in_bf)
    return u, xn


def _lru_kernel(cur_ref, prev_ref, next_ref, cw_ref, cb_ref, wg_ref, bg_ref, lam_ref, h_ref,
                a_scr, u_scr, cin_scr, c_scr, *, reverse, tc):
    t = pl.program_id(1)
    nt = pl.num_programs(1)
    tt = (nt - 1 - t) if reverse else t
    w = LRU_WIDTH
    ng = tc // SUBLANES

    @pl.when(t == 0)
    def _():
        c_scr[...] = jnp.zeros_like(c_scr)

    cur = cur_ref[...].reshape(ng, SUBLANES, w)
    prev = jnp.where(tt > 0, prev_ref[...], 0.0)
    nxt = jnp.where(tt < nt - 1, next_ref[...], 0.0)
    rm = lax.broadcasted_iota(jnp.int32, (ng, SUBLANES, w), 1)

    def shift_down(k):
        s = pltpu.roll(cur, k, axis=1)
        before = jnp.concatenate([pltpu.roll(prev, k, axis=0)[None], s[:ng - 1]], axis=0)
        return jnp.where(rm < k, before, s)

    s_up = pltpu.roll(cur, SUBLANES - 1, axis=1)
    after = jnp.concatenate([s_up[1:], pltpu.roll(nxt, SUBLANES - 1, axis=0)[None]], axis=0)
    x_p1 = jnp.where(rm == SUBLANES - 1, after, s_up)

    cw = cw_ref[...]
    xc = shift_down(2) * cw[0:1] + shift_down(1) * cw[1:2]
    xc = xc + cur * cw[2:3]
    xc = xc + x_p1 * cw[3:4]
    xc = (xc + cb_ref[...]).reshape(tc, w)

    z = jnp.dot(xc.astype(BF16), wg_ref[...], preferred_element_type=F32) + bg_ref[...]
    r = _sigmoid(z[:, :w])
    ig = _sigmoid(z[:, w:])
    lam = lam_ref[...]
    e = jnp.exp(-jnp.abs(lam))
    e1 = 1.0 + e
    d = e1 - 1.0
    log1p_e = jnp.where(d == 0.0, e, jnp.log(e1) * (e / jnp.where(d == 0.0, 1.0, d)))
    a = jnp.exp2(((LRU_C * LOG2_E) * (jnp.minimum(lam, 0.0) - log1p_e)) * r)
    z1 = 1.0 - a * a
    u = (z1 * lax.rsqrt(jnp.maximum(z1, F32_TINY))) * (ig * xc)

    a = a.reshape(ng, SUBLANES, w)
    u = u.reshape(ng, SUBLANES, w)
    for k in (1, 2, 4):
        if reverse:
            keep = rm < SUBLANES - k
            shift = SUBLANES - k
        else:
            keep = rm >= k
            shift = k
        a_s = jnp.where(keep, pltpu.roll(a, shift, axis=1), 1.0)
        u_s = jnp.where(keep, pltpu.roll(u, shift, axis=1), 0.0)
        u = u + a * u_s
        a = a * a_s
    a_scr[...] = a.reshape(tc, w)
    u_scr[...] = u.reshape(tc, w)

    edge = 0 if reverse else SUBLANES - 1

    def chain(i, c):
        g = (ng - 1 - i) if reverse else i
        cin_scr[pl.ds(g, 1), :] = c
        r = g * SUBLANES + edge
        return u_scr[pl.ds(r, 1), :] + a_scr[pl.ds(r, 1), :] * c

    c_scr[...] = lax.fori_loop(0, ng, chain, c_scr[...], unroll=8)

    def apply(g, carry):
        r0 = pl.multiple_of(g * SUBLANES, SUBLANES)
        h_ref[pl.ds(r0, SUBLANES), :] = (u_scr[pl.ds(r0, SUBLANES), :]
                                         + a_scr[pl.ds(r0, SUBLANES), :] * cin_scr[pl.ds(g, 1), :])
        return carry

    lax.fori_loop(0, ng, apply, 0, unroll=8)


def _lru_scan(u3, conv_w, conv_b, w_gates_bf, b_gates, lam, reverse):
    bsz, n, _ = u3.shape
    w = LRU_WIDTH
    tc = min(LRU_TIME_TILE, n)
    nt = n // tc
    nb8 = n // SUBLANES
    per = tc // SUBLANES

    def tmap(t):
        return (nt - 1 - t) if reverse else t

    in_specs = [
        pl.BlockSpec((None, tc, w), lambda b, t: (b, tmap(t), COL_LRU_X)),
        pl.BlockSpec((None, SUBLANES, w), lambda b, t: (b, jnp.maximum(tmap(t) * per - 1, 0), COL_LRU_X)),
        pl.BlockSpec((None, SUBLANES, w), lambda b, t: (b, jnp.minimum((tmap(t) + 1) * per, nb8 - 1), COL_LRU_X)),
        pl.BlockSpec((CONV_W, w), lambda b, t: (0, 0)),
        pl.BlockSpec((1, w), lambda b, t: (0, 0)),
        pl.BlockSpec((w, 2 * w), lambda b, t: (0, 0)),
        pl.BlockSpec((1, 2 * w), lambda b, t: (0, 0)),
        pl.BlockSpec((1, w), lambda b, t: (0, 0)),
    ]
    out_spec = pl.BlockSpec((None, tc, w), lambda b, t: (b, tmap(t), 0))
    return pl.pallas_call(
        functools.partial(_lru_kernel, reverse=reverse, tc=tc),
        grid=(bsz, nt), in_specs=in_specs, out_specs=out_spec,
        out_shape=jax.ShapeDtypeStruct((bsz, n, w), F32),
        scratch_shapes=[pltpu.VMEM((tc, w), F32), pltpu.VMEM((tc, w), F32), pltpu.VMEM((tc // SUBLANES, w), F32),
                        pltpu.VMEM((1, w), F32)],
        compiler_params=_params(2), name="lru_bwd" if reverse else "lru_fwd",
    )(u3, u3, u3, conv_w, conv_b.reshape(1, w), w_gates_bf, b_gates.reshape(1, 2 * w), lam.reshape(1, w))


def _hgrn_kernel(q_ref, v_ref, f_ref, lb_ref, o_ref, st_scr, *, reverse, layer, tb, c):
    t = pl.program_id(1)

    @pl.when(t == 0)
    def _():
        st_scr[...] = jnp.zeros_like(st_scr)

    z = lb_ref[...]
    e = jnp.exp(z - jnp.max(z, axis=0, keepdims=True))
    p = e / jnp.sum(e, axis=0, keepdims=True)
    lb_all = jnp.sum(p[0:layer + 1], axis=0, keepdims=True) - p[0:1]

    ri = lax.broadcasted_iota(jnp.int32, (c, c), 0)
    ci = lax.broadcasted_iota(jnp.int32, (c, c), 1)
    valid = (ci >= ri) if reverse else (ci <= ri)
    tri = jnp.where(valid, 1.0, 0.0).astype(BF16)
    half = c // 2
    scale = HG_DK ** -0.5
    nchunks = tb // c
    nt_dims = (((1,), (1,)), ((), ()))
    tn_dims = (((0,), (0,)), ((), ()))

    order = [(j, h) for j in (range(nchunks - 1, -1, -1) if reverse else range(nchunks)) for h in range(HG_HEADS)]
    q_all, k_all, v_all, cum_all = {}, {}, {}, {}
    for j, h in order:
        sl = slice(j * c, (j + 1) * c)
        hl = slice(h * HG_DK, (h + 1) * HG_DK)
        lb = lb_all[:, hl]
        q_all[j, h] = _silu(q_ref[sl, hl]) * scale
        v_all[j, h] = v_ref[sl, hl].astype(BF16)
        f = lb + (1.0 - lb) * _sigmoid(f_ref[sl, hl])
        k_all[j, h] = 1.0 - f
        g = jnp.log2(f)
        g_hi = g.astype(BF16)
        r1 = g - g_hi.astype(F32)
        g_mid = r1.astype(BF16)
        g_lo = (r1 - g_mid.astype(F32)).astype(BF16)
        cum3 = jnp.dot(tri, jnp.concatenate([g_hi, g_mid, g_lo], axis=1), preferred_element_type=F32)
        cum_all[j, h] = (cum3[:, 0:HG_DK] + cum3[:, HG_DK:2 * HG_DK]) + cum3[:, 2 * HG_DK:3 * HG_DK]

    scores_all, qs_all, ds_all, dec_all = {}, {}, {}, {}
    for j, h in order:
        q, k, cum = q_all[j, h], k_all[j, h], cum_all[j, h]
        if reverse:
            mid = cum[half:half + 1]
            tot = cum[0:1]
        else:
            mid = cum[half - 1:half]
            tot = cum[c - 1:c]
        qt = q * jnp.exp2(jnp.minimum(cum - mid, HG_EXP2_CLAMP))
        kt = k * jnp.exp2(jnp.minimum(mid - cum, HG_EXP2_CLAMP))
        scores = lax.dot_general(qt.astype(BF16), kt.astype(BF16), nt_dims, preferred_element_type=F32)
        scores_all[j, h] = jnp.where(valid, scores, 0.0).astype(BF16)
        qs_all[j, h] = (q * jnp.exp2(cum)).astype(BF16)
        ks = (k * jnp.exp2(tot - cum)).astype(BF16)
        ds_all[j, h] = lax.dot_general(v_all[j, h], ks, tn_dims, preferred_element_type=F32)
        dec_all[j, h] = jnp.exp2(tot)

    intra_all = {}
    for j, h in order:
        intra_all[j, h] = jnp.dot(scores_all[j, h], v_all[j, h], preferred_element_type=F32)

    states = [st_scr[h] for h in range(HG_HEADS)]
    for j, h in order:
        st = states[h]
        o = intra_all[j, h] + lax.dot_general(qs_all[j, h], st.astype(BF16), nt_dims, preferred_element_type=F32)
        o_ref[j * c:(j + 1) * c, h * HG_DV:(h + 1) * HG_DV] = o
        states[h] = st * dec_all[j, h] + ds_all[j, h]
    for h in range(HG_HEADS):
        st_scr[h] = states[h]


def _hgrn_scan(u3, lb_raw, layer, reverse):
    bsz, n, _ = u3.shape
    depth = lb_raw.shape[0]
    tb = min(HG_TIME_TILE, n)
    nt = n // tb
    c = min(HG_CHUNK, tb)
    assert n % tb == 0 and tb % c == 0
    f_col = COL_F_BWD if reverse else COL_F_FWD

    def tmap(t):
        return (nt - 1 - t) if reverse else t

    in_specs = [
        pl.BlockSpec((None, tb, HG_W), lambda b, t: (b, tmap(t), COL_Q)),
        pl.BlockSpec((None, tb, HG_W), lambda b, t: (b, tmap(t), COL_I)),
        pl.BlockSpec((None, tb, HG_W), lambda b, t: (b, tmap(t), f_col)),
        pl.BlockSpec((depth, HG_W), lambda b, t: (0, 0)),
    ]
    out_spec = pl.BlockSpec((None, tb, HG_W), lambda b, t: (b, tmap(t), 0))
    return pl.pallas_call(
        functools.partial(_hgrn_kernel, reverse=reverse, layer=layer, tb=tb, c=c),
        grid=(bsz, nt), in_specs=in_specs, out_specs=out_spec,
        out_shape=jax.ShapeDtypeStruct((bsz, n, HG_W), F32),
        scratch_shapes=[pltpu.VMEM((HG_HEADS, HG_DV, HG_DK), F32)],
        compiler_params=_params(2), name="hgrn_bwd" if reverse else "hgrn_fwd",
    )(u3, u3, u3, lb_raw)


def _route(logits_t, b_router):
    mx = jnp.max(logits_t, axis=0, keepdims=True)
    ex = jnp.exp(logits_t - mx)
    probs = ex / jnp.sum(ex, axis=0, keepdims=True)
    sel = probs + b_router
    rows = [sel[i:i + 1] for i in range(N_EXPERTS)]
    prow = [probs[i:i + 1] for i in range(N_EXPERTS)]
    gscore = []
    for gi in range(N_GROUPS):
        m = rows[gi * EXPERTS_PER_GROUP:(gi + 1) * EXPERTS_PER_GROUP]
        best = None
        for a in range(EXPERTS_PER_GROUP):
            for b in range(a + 1, EXPERTS_PER_GROUP):
                s = m[a] + m[b]
                best = s if best is None else jnp.maximum(best, s)
        gscore.append(best)
    gbest = jnp.zeros_like(gscore[0], dtype=jnp.int32)
    gmax = gscore[0]
    for gi in range(1, N_GROUPS):
        better = gscore[gi] > gmax
        gbest = jnp.where(better, gi, gbest)
        gmax = jnp.where(better, gscore[gi], gmax)
    ms, mp = [], []
    for j in range(EXPERTS_PER_GROUP):
        s = rows[j]
        pr = prow[j]
        for gi in range(1, N_GROUPS):
            pick = gbest == gi
            s = jnp.where(pick, rows[gi * EXPERTS_PER_GROUP + j], s)
            pr = jnp.where(pick, prow[gi * EXPERTS_PER_GROUP + j], pr)
        ms.append(s)
        mp.append(pr)
    i1 = jnp.zeros_like(gbest)
    v1 = ms[0]
    for j in range(1, EXPERTS_PER_GROUP):
        better = ms[j] > v1
        i1 = jnp.where(better, j, i1)
        v1 = jnp.where(better, ms[j], v1)
    i2 = jnp.full_like(gbest, -1)
    v2 = jnp.full_like(v1, -jnp.inf)
    for j in range(EXPERTS_PER_GROUP):
        better = (i1 != j) & ((ms[j] > v2) | (i2 < 0))
        i2 = jnp.where(better, j, i2)
        v2 = jnp.where(better, ms[j], v2)
    p1 = mp[0]
    p2 = mp[0]
    for j in range(1, EXPERTS_PER_GROUP):
        p1 = jnp.where(i1 == j, mp[j], p1)
        p2 = jnp.where(i2 == j, mp[j], p2)
    den = p1 + p2
    w1 = p1 / den
    w2 = p2 / den
    first_low = i1 < i2
    lo = jnp.where(first_low, i1, i2)
    hi = jnp.where(first_low, i2, i1)
    pair = jnp.where(lo == 0, hi - 1, jnp.where(lo == 1, hi + 1, 5))
    cls = gbest * N_PAIRS + pair
    return cls, jnp.where(first_low, w1, w2), jnp.where(first_low, w2, w1)


def _mix_kernel(x_ref, hf_ref, hb_ref, ug_ref, of_ref, ob_ref, uhg_ref, ng_ref, wo_ref, g1_ref, b1_ref,
                wr_ref, br_ref, x1w_ref, cls_ref, rank_ref, cnt_ref, carry_scr, *, alpha):
    tm = x_ref.shape[0]

    @pl.when(pl.program_id(0) == 0)
    def _():
        carry_scr[...] = jnp.zeros_like(carry_scr)

    lru = (hf_ref[...] + hb_ref[...]) * jax.nn.gelu(ug_ref[...], approximate=True)
    o = of_ref[...] + ob_ref[...]
    heads = []
    for h in range(HG_HEADS):
        oh = o[:, h * HG_DV:(h + 1) * HG_DV]
        heads.append(oh * lax.rsqrt(jnp.mean(oh * oh, axis=-1, keepdims=True) + RMS_EPS))
    hg = jnp.concatenate(heads, axis=1) * ng_ref[...] * _silu(uhg_ref[...])
    mix_in = jnp.concatenate([lru, hg], axis=1).astype(BF16)
    mix = jnp.dot(mix_in, wo_ref[...], preferred_element_type=F32)
    x1 = _layer_norm(alpha * x_ref[...] + mix, g1_ref[...], b1_ref[...])
    def split(v):
        hi = v.astype(BF16)
        return hi, (v - hi.astype(F32)).astype(BF16)

    def dot_nt(a, b):
        return lax.dot_general(a, b, (((1,), (1,)), ((), ())), preferred_element_type=F32)

    wr_hi, wr_lo = split(wr_ref[...])
    x_hi, x_lo = split(x1)
    logits_t = dot_nt(wr_hi, x_hi) + (dot_nt(wr_hi, x_lo) + dot_nt(wr_lo, x_hi))
    cls, w_lo, w_hi = _route(logits_t, br_ref[...])
    rid = lax.broadcasted_iota(jnp.int32, (LANES, tm), 0)
    route_t = jnp.where(rid == 0, w_lo, jnp.where(rid == 1, w_hi, 0.0))
    x1w_ref[...] = jnp.concatenate([x1, route_t.T], axis=1)
    cid = lax.broadcasted_iota(jnp.int32, (CLASS_ROWS, tm), 0)
    onehot = jnp.where(cid == cls, 1.0, 0.0)
    si = lax.broadcasted_iota(jnp.int32, (tm, tm), 0)
    ti = lax.broadcasted_iota(jnp.int32, (tm, tm), 1)
    upper = jnp.where(si <= ti, 1.0, 0.0).astype(BF16)
    prefix = jnp.dot(onehot.astype(BF16), upper, preferred_element_type=F32)
    carry = carry_scr[...]
    rank = jnp.sum(onehot * (prefix - 1.0 + carry), axis=0, keepdims=True)
    cls_ref[...] = cls
    rank_ref[...] = rank.astype(jnp.int32)
    carry = carry + prefix[:, tm - 1:tm]
    carry_scr[...] = carry
    cnt_ref[...] = jnp.broadcast_to(carry, cnt_ref.shape)


def _mix(x2d, hf, hb, u2d, of, ob, norm_g, w_out_bf, ln_g, ln_b, w_router_t, b_router, alpha):
    t, d = x2d.shape
    w = LRU_WIDTH
    tm = min(TOKEN_TILE, t)
    assert t % tm == 0
    nt = t // tm
    row = lambda i: (i, 0)
    const = lambda i: (0, 0)
    in_specs = [
        pl.BlockSpec((tm, d), row),
        pl.BlockSpec((tm, w), row),
        pl.BlockSpec((tm, w), row),
        pl.BlockSpec((tm, w), lambda i: (i, COL_LRU_GATE)),
        pl.BlockSpec((tm, w), row),
        pl.BlockSpec((tm, w), row),
        pl.BlockSpec((tm, w), lambda i: (i, COL_G)),
        pl.BlockSpec((1, w), const),
        pl.BlockSpec((2 * w, d), const),
        pl.BlockSpec((1, d), const),
        pl.BlockSpec((1, d), const),
        pl.BlockSpec((N_EXPERTS, d), const),
        pl.BlockSpec((N_EXPERTS, 1), const),
    ]
    tok_spec = pl.BlockSpec((None, 1, tm), lambda i: (i, 0, 0))
    out_specs = [pl.BlockSpec((tm, d + LANES), row), tok_spec, tok_spec, pl.BlockSpec((CLASS_ROWS, LANES), const)]
    x1w, cls, rank, cnt = pl.pallas_call(
        functools.partial(_mix_kernel, alpha=alpha),
        grid=(nt,), in_specs=in_specs, out_specs=out_specs,
        out_shape=[jax.ShapeDtypeStruct((t, d + LANES), F32), jax.ShapeDtypeStruct((nt, 1, tm), jnp.int32),
                   jax.ShapeDtypeStruct((nt, 1, tm), jnp.int32), jax.ShapeDtypeStruct((CLASS_ROWS, LANES), F32)],
        scratch_shapes=[pltpu.VMEM((CLASS_ROWS, 1), F32)],
        compiler_params=_params(1), name="mix_ln_route",
    )(x2d, hf, hb, u2d, of, ob, u2d, norm_g.reshape(1, w), w_out_bf, ln_g.reshape(1, d), ln_b.reshape(1, d),
      w_router_t, b_router.reshape(N_EXPERTS, 1))
    return x1w, cls.reshape(t), rank.reshape(t), cnt[:N_CLASSES, 0].astype(jnp.int32)


def _route_plan(cls, rank, counts, t, tm):
    nt = t // tm + N_CLASSES
    tiles_c = (counts + tm - 1) // tm
    tile_end = jnp.cumsum(tiles_c)
    pos = ((tile_end - tiles_c) * tm)[cls] + rank
    last_tile = jnp.where(tiles_c > 0, tile_end - 1, -1).astype(jnp.int32)
    used = tile_end[-1:].astype(jnp.int32)
    xt = jnp.minimum(jnp.arange(nt, dtype=jnp.int32), tile_end[-1] - 1)
    tile_cls = jnp.sum((tile_end[None, :] <= xt[:, None]).astype(jnp.int32), axis=1)
    group = tile_cls // N_PAIRS
    pair = tile_cls % N_PAIRS
    ea = group * EXPERTS_PER_GROUP + jnp.array(PAIR_LO, jnp.int32)[pair]
    eb = group * EXPERTS_PER_GROUP + jnp.array(PAIR_HI, jnp.int32)[pair]
    return pos, last_tile, used, xt, ea, eb


def _dispatch_kernel(last_ref, used_ref, pos_ref, x_ref, xs_hbm, zbuf, xbuf, sem_z, sem, *, td, tm, n_tiles):
    i = pl.program_id(0)
    slot = lax.rem(i, 2)

    def zero_tile(j):
        cp = pltpu.make_async_copy(zbuf, xs_hbm.at[pl.ds(pl.multiple_of(j * tm, tm), tm)], sem_z.at[0])
        cp.start()
        cp.wait()

    @pl.when(i == 0)
    def _():
        zbuf[...] = jnp.zeros_like(zbuf)
        for c in range(N_CLASSES):
            @pl.when(last_ref[c] >= 0)
            def _():
                zero_tile(last_ref[c])

        def tail(j, carry):
            zero_tile(j)
            return carry
        lax.fori_loop(used_ref[0], n_tiles, tail, 0)

    def wait_slot(s):
        pltpu.make_async_copy(xbuf.at[s], xs_hbm.at[pl.ds(0, td)], sem.at[s]).wait()

    @pl.when(i >= 2)
    def _():
        wait_slot(slot)

    xbuf[slot] = x_ref[...]
    for r in range(td):
        pltpu.make_async_copy(xbuf.at[slot, pl.ds(r, 1)], xs_hbm.at[pl.ds(pos_ref[0, r], 1)], sem.at[slot]).start()

    @pl.when(i == pl.num_programs(0) - 1)
    def _():
        wait_slot(slot)

        @pl.when(i >= 1)
        def _():
            wait_slot(1 - slot)


def _dispatch(x1w, pos, last_tile, used, tm, n_tiles):
    t, w = x1w.shape
    nt, _, td = pos.shape
    grid_spec = pltpu.PrefetchScalarGridSpec(
        num_scalar_prefetch=2, grid=(nt,),
        in_specs=[pl.BlockSpec((None, 1, td), lambda i, lt, us: (i, 0, 0), memory_space=pltpu.SMEM),
                  pl.BlockSpec((td, w), lambda i, lt, us: (i, 0))],
        out_specs=pl.BlockSpec(memory_space=pl.ANY),
        scratch_shapes=[pltpu.VMEM((tm, w), F32), pltpu.VMEM((2, td, w), F32),
                        pltpu.SemaphoreType.DMA((1,)), pltpu.SemaphoreType.DMA((2,))])
    return pl.pallas_call(
        functools.partial(_dispatch_kernel, td=td, tm=tm, n_tiles=n_tiles), grid_spec=grid_spec,
        out_shape=jax.ShapeDtypeStruct((n_tiles * tm, w), F32),
        compiler_params=_params(1), name="dispatch",
    )(last_tile, used, pos, x1w)


def _start_row_gather(idx_ref, src_hbm, buf, sem, slot, rows):
    for r in range(rows):
        pltpu.make_async_copy(src_hbm.at[pl.ds(idx_ref[0, r], 1)], buf.at[slot, pl.ds(r, 1)], sem.at[slot]).start()


def _wait_row_gather(src_hbm, buf, sem, slot, rows):
    pltpu.make_async_copy(src_hbm.at[pl.ds(0, rows)], buf.at[slot], sem.at[slot]).wait()


def _moe_kernel(xt_ref, ea_ref, eb_ref, x_ref, wga_ref, wua_ref, wda_ref, wgb_ref, wub_ref, wdb_ref, y_ref, *, d):
    del xt_ref, ea_ref, eb_ref
    xw = x_ref[...]
    x = xw[:, :d].astype(BF16)
    w_lo = xw[:, d:d + 1]
    w_hi = xw[:, d + 1:d + 2]

    def expert(wg_ref, wu_ref, wd_ref):
        hid = _silu(jnp.dot(x, wg_ref[...], preferred_element_type=F32)) * jnp.dot(
            x, wu_ref[...], preferred_element_type=F32)
        return jnp.dot(hid.astype(BF16), wd_ref[...], preferred_element_type=F32)

    y_ref[...] = w_lo * expert(wga_ref, wua_ref, wda_ref) + w_hi * expert(wgb_ref, wub_ref, wdb_ref)


def _moe(xs, xt, ea, eb, tm, wg_bf, wu_bf, wd_bf):
    d = xs.shape[1] - LANES
    _, _, dff = wg_bf.shape
    nt = xs.shape[0] // tm
    up_a = pl.BlockSpec((None, d, dff), lambda i, xt, ea, eb: (ea[i], 0, 0))
    down_a = pl.BlockSpec((None, dff, d), lambda i, xt, ea, eb: (ea[i], 0, 0))
    up_b = pl.BlockSpec((None, d, dff), lambda i, xt, ea, eb: (eb[i], 0, 0))
    down_b = pl.BlockSpec((None, dff, d), lambda i, xt, ea, eb: (eb[i], 0, 0))
    grid_spec = pltpu.PrefetchScalarGridSpec(
        num_scalar_prefetch=3, grid=(nt,),
        in_specs=[pl.BlockSpec((tm, d + LANES), lambda i, xt, ea, eb: (xt[i], 0)),
                  up_a, up_a, down_a, up_b, up_b, down_b],
        out_specs=pl.BlockSpec((tm, d), lambda i, xt, ea, eb: (i, 0)))
    return pl.pallas_call(
        functools.partial(_moe_kernel, d=d), grid_spec=grid_spec,
        out_shape=jax.ShapeDtypeStruct((nt * tm, d), F32),
        compiler_params=_params(1), name="moe_routed",
    )(xt, ea, eb, xs, wg_bf, wu_bf, wd_bf, wg_bf, wu_bf, wd_bf)


def _ln2_gathered(pos_ref, posn_ref, x1_ref, ys_hbm, g_ref, b_ref, ybuf, sem, alpha):
    i = pl.program_id(0)
    tm = x1_ref.shape[0]
    slot = lax.rem(i, 2)

    @pl.when(i == 0)
    def _():
        _start_row_gather(pos_ref, ys_hbm, ybuf, sem, 0, tm)

    _wait_row_gather(ys_hbm, ybuf, sem, slot, tm)
    _start_row_gather(posn_ref, ys_hbm, ybuf, sem, 1 - slot, tm)
    return _layer_norm(alpha * x1_ref[...] + ybuf[slot], g_ref[...], b_ref[...])


def _ln2_finish(ys_hbm, ybuf, sem):
    i = pl.program_id(0)

    @pl.when(i == pl.num_programs(0) - 1)
    def _():
        _wait_row_gather(ys_hbm, ybuf, sem, 1 - lax.rem(i, 2), ybuf.shape[1])


def _ln2_in_proj_kernel(pos_ref, posn_ref, x1_ref, ys_hbm, g_ref, b_ref, w_ref, u_ref, x2_ref, ybuf, sem, *, alpha):
    x2 = _ln2_gathered(pos_ref, posn_ref, x1_ref, ys_hbm, g_ref, b_ref, ybuf, sem, alpha)
    x2_ref[...] = x2
    u_ref[...] = jnp.dot(x2.astype(BF16), w_ref[...], preferred_element_type=F32)
    _ln2_finish(ys_hbm, ybuf, sem)


def _ln2_kernel(pos_ref, posn_ref, x1_ref, ys_hbm, g_ref, b_ref, x2_ref, ybuf, sem, *, alpha):
    x2_ref[...] = _ln2_gathered(pos_ref, posn_ref, x1_ref, ys_hbm, g_ref, b_ref, ybuf, sem, alpha)
    _ln2_finish(ys_hbm, ybuf, sem)


def _ln2(x1w, ys, pos, ln_g, ln_b, alpha, w_in_bf=None):
    t = x1w.shape[0]
    d = ys.shape[1]
    nt, _, tm = pos.shape
    row = lambda i: (i, 0)
    const = lambda i: (0, 0)
    x_spec = pl.BlockSpec((tm, d), row)
    vec_spec = pl.BlockSpec((1, d), const)
    in_specs = [pl.BlockSpec((None, 1, tm), lambda i: (i, 0, 0), memory_space=pltpu.SMEM),
                pl.BlockSpec((None, 1, tm), lambda i: (jnp.minimum(i + 1, nt - 1), 0, 0), memory_space=pltpu.SMEM),
                x_spec, pl.BlockSpec(memory_space=pl.ANY), vec_spec, vec_spec]
    scratch = [pltpu.VMEM((2, tm, d), F32), pltpu.SemaphoreType.DMA((2,))]
    args = (pos, pos, x1w, ys, ln_g.reshape(1, d), ln_b.reshape(1, d))
    if w_in_bf is None:
        return pl.pallas_call(
            functools.partial(_ln2_kernel, alpha=alpha), grid=(nt,), in_specs=in_specs, out_specs=x_spec,
            out_shape=jax.ShapeDtypeStruct((t, d), F32), scratch_shapes=scratch,
            compiler_params=_params(1), name="ln2")(*args)
    cols = w_in_bf.shape[1]
    return pl.pallas_call(
        functools.partial(_ln2_in_proj_kernel, alpha=alpha), grid=(nt,),
        in_specs=in_specs + [pl.BlockSpec((d, cols), const)],
        out_specs=[pl.BlockSpec((tm, cols), row), x_spec],
        out_shape=[jax.ShapeDtypeStruct((t, cols), F32), jax.ShapeDtypeStruct((t, d), F32)],
        scratch_shapes=scratch, compiler_params=_params(1), name="ln2_in_proj")(*args, w_in_bf)


def _block_diag(blocks):
    nb, bw, _ = blocks.shape
    eye = jnp.eye(nb, dtype=blocks.dtype)
    return jnp.einsum("hij,hg->higj", blocks, eye).reshape(nb * bw, nb * bw)


def _trunk(x, p):
    bsz, n, d = x.shape
    depth = p["w_in"].shape[0]
    alpha = (2 * depth) ** 0.25
    t = bsz * n
    u2d, x2d = _in_proj(x.reshape(t, d), p["w_in_bf"][0], (p["ln_in_g"], p["ln_in_b"]))
    for l in range(depth):
        u3 = u2d.reshape(bsz, n, -1)
        hs, os_ = [], []
        for di, rev in enumerate((False, True)):
            hs.append(_lru_scan(u3, p["conv_w"][l], p["conv_b"][l], p["w_gates_bf"][l][di], p["b_gates"][l][di],
                                p["lru_lambda"][l][di], rev))
            os_.append(_hgrn_scan(u3, p["hgrn_lb"][di], l, rev))
        x1w, cls, rank, counts = _mix(
            x2d, hs[0].reshape(t, -1), hs[1].reshape(t, -1), u2d, os_[0].reshape(t, -1), os_[1].reshape(t, -1),
            p["hgrn_norm_g"][l], p["w_out_bf"][l], p["ln1_g"][l], p["ln1_b"][l], p["w_router_t"], p["b_router"], alpha)
        tile_rows = min(MOE_ROW_TILE, t)
        pos, last_tile, used, xt, ea, eb = _route_plan(cls, rank, counts, t, tile_rows)
        tm = min(TOKEN_TILE, t)
        pos = pos.reshape(t // tm, 1, tm)
        xs = _dispatch(x1w, pos, last_tile, used, tile_rows, xt.shape[0])
        ys = _moe(xs, xt, ea, eb, tile_rows, p["w_gate_bf"][l], p["w_up_bf"][l], p["w_down_bf"][l])
        if l + 1 < depth:
            u2d, x2d = _ln2(x1w, ys, pos, p["ln2_g"][l], p["ln2_b"][l], alpha, p["w_in_bf"][l + 1])
        else:
            x2d = _ln2(x1w, ys, pos, p["ln2_g"][l], p["ln2_b"][l], alpha)
    return x2d.reshape(bsz, n, d)


def kernel(x_prompt, x_sample, ln_in_g, ln_in_b, w_in, conv_w, conv_b, lru_wa, lru_ba, lru_wx, lru_bx, lru_lambda,
           hgrn_lb, hgrn_norm_g, w_out, ln1_g, ln1_b, w_router, b_router, w_gate, w_up, w_down, ln2_g, ln2_b):
    depth = w_in.shape[0]
    d = w_in.shape[1]
    w_gates = jnp.stack([
        jnp.stack([jnp.concatenate([_block_diag(lru_wa[l, di]), _block_diag(lru_wx[l, di])], axis=1)
                   for di in range(2)]) for l in range(depth)])
    b_gates = jnp.concatenate([lru_ba, lru_bx], axis=-1)
    w_router_t = w_router.T
    p = dict(
        ln_in_g=ln_in_g, ln_in_b=ln_in_b, w_in_bf=w_in.astype(BF16), conv_w=conv_w, conv_b=conv_b,
        w_gates_bf=w_gates.astype(BF16), b_gates=b_gates, lru_lambda=lru_lambda, hgrn_lb=hgrn_lb,
        hgrn_norm_g=hgrn_norm_g, w_out_bf=w_out.astype(BF16), ln1_g=ln1_g, ln1_b=ln1_b,
        w_router_t=w_router_t, b_router=b_router, w_gate_bf=w_gate.astype(BF16), w_up_bf=w_up.astype(BF16),
        w_down_bf=w_down.astype(BF16), ln2_g=ln2_g, ln2_b=ln2_b, w_in=w_in)
    return (_trunk(x_prompt, p), _trunk(x_sample, p))
```

```python
import functools

import jax
import jax.numpy as jnp
from jax import lax
from jax.experimental import pallas as pl
from jax.experimental.pallas import tpu as pltpu

F32 = jnp.float32
BF16 = jnp.bfloat16

LRU_WIDTH = 512
LRU_BLOCKS = 8
CONV_W = 4
CONV_LEFT = 2
LRU_C = 8.0
HG_HEADS = 4
HG_DK = 128
HG_DV = 128
HG_W = HG_HEADS * HG_DK
N_EXPERTS = 16
N_GROUPS = 4
EXPERTS_PER_GROUP = N_EXPERTS // N_GROUPS
PAIR_LO = (0, 0, 0, 1, 1, 2)
PAIR_HI = (1, 2, 3, 2, 3, 3)
N_PAIRS = len(PAIR_LO)
N_CLASSES = N_GROUPS * N_PAIRS
CLASS_ROWS = 32
LN_EPS = 1e-5
RMS_EPS = 1e-6
F32_TINY = 1.1754944e-38
LOG2_E = 1.4426950408889634
COL_LRU_X, COL_LRU_GATE, COL_Q, COL_I, COL_F_FWD, COL_F_BWD, COL_G = range(7)

LANES = 128
SUBLANES = 8
VMEM_LIMIT_BYTES = 56 * 1024 * 1024

TOKEN_TILE = 512
LRU_TIME_TILE = 512
HG_TIME_TILE = 512
HG_CHUNK = 64
MOE_ROW_TILE = 256
HG_EXP2_CLAMP = 115.0


def _params(n_axes):
    return pltpu.CompilerParams(dimension_semantics=("arbitrary",) * n_axes,
                                vmem_limit_bytes=VMEM_LIMIT_BYTES)


def _layer_norm(x, g, b):
    mu = jnp.mean(x, axis=-1, keepdims=True)
    xc = x - mu
    var = jnp.mean(xc * xc, axis=-1, keepdims=True)
    return xc * lax.rsqrt(var + LN_EPS) * g + b


def _sigmoid(x):
    return 0.5 + 0.5 * jnp.tanh(0.5 * x)


def _silu(x):
    return x * _sigmoid(x)


def _in_proj_ln_kernel(x_ref, g_ref, b_ref, w_ref, u_ref, xn_ref):
    xn = _layer_norm(x_ref[...], g_ref[...], b_ref[...])
    xn_ref[...] = xn
    u_ref[...] = jnp.dot(xn.astype(BF16), w_ref[...], preferred_element_type=F32)


def _in_proj_kernel(x_ref, w_ref, u_ref):
    u_ref[...] = jnp.dot(x_ref[...].astype(BF16), w_ref[...], preferred_element_type=F32)


def _in_proj(x2d, w_in_bf, ln=None):
    t, d = x2d.shape
    cols = w_in_bf.shape[1]
    tm = min(TOKEN_TILE, t)
    grid = (t // tm,)
    x_spec = pl.BlockSpec((tm, d), lambda i: (i, 0))
    w_spec = pl.BlockSpec((d, cols), lambda i: (0, 0))
    u_spec = pl.BlockSpec((tm, cols), lambda i: (i, 0))
    vec_spec = pl.BlockSpec((1, d), lambda i: (0, 0))
    if ln is None:
        u = pl.pallas_call(
            _in_proj_kernel, grid=grid, in_specs=[x_spec, w_spec], out_specs=u_spec,
            out_shape=jax.ShapeDtypeStruct((t, cols), F32), compiler_params=_params(1),
            name="in_proj")(x2d, w_in_bf)
        return u, x2d
    g, b = ln
    u, xn = pl.pallas_call(
        _in_proj_ln_kernel, grid=grid, in_specs=[x_spec, vec_spec, vec_spec, w_spec],
        out_specs=[u_spec, x_spec],
        out_shape=[jax.ShapeDtypeStruct((t, cols), F32), jax.ShapeDtypeStruct((t, d), F32)],
        compiler_params=_params(1), name="in_proj_ln")(x2d, g.reshape(1, d), b.reshape(1, d), w_in_bf)
    return u, xn


def _lru_kernel(cur_ref, prev_ref, next_ref, cw_ref, cb_ref, wg_ref, bg_ref, lam_ref, h_ref,
                a_scr, u_scr, cin_scr, c_scr, *, reverse, tc):
    t = pl.program_id(1)
    nt = pl.num_programs(1)
    tt = (nt - 1 - t) if reverse else t
    w = LRU_WIDTH
    ng = tc // SUBLANES

    @pl.when(t == 0)
    def _():
        c_scr[...] = jnp.zeros_like(c_scr)

    cur = cur_ref[...].reshape(ng, SUBLANES, w)
    prev = jnp.where(tt > 0, prev_ref[...], 0.0)
    nxt = jnp.where(tt < nt - 1, next_ref[...], 0.0)
    rm = lax.broadcasted_iota(jnp.int32, (ng, SUBLANES, w), 1)

    def shift_down(k):
        s = pltpu.roll(cur, k, axis=1)
        before = jnp.concatenate([pltpu.roll(prev, k, axis=0)[None], s[:ng - 1]], axis=0)
        return jnp.where(rm < k, before, s)

    s_up = pltpu.roll(cur, SUBLANES - 1, axis=1)
    after = jnp.concatenate([s_up[1:], pltpu.roll(nxt, SUBLANES - 1, axis=0)[None]], axis=0)
    x_p1 = jnp.where(rm == SUBLANES - 1, after, s_up)

    cw = cw_ref[...]
    xc = shift_down(2) * cw[0:1] + shift_down(1) * cw[1:2]
    xc = xc + cur * cw[2:3]
    xc = xc + x_p1 * cw[3:4]
    xc = (xc + cb_ref[...]).reshape(tc, w)

    z = jnp.dot(xc.astype(BF16), wg_ref[...], preferred_element_type=F32) + bg_ref[...]
    r = _sigmoid(z[:, :w])
    ig = _sigmoid(z[:, w:])
    lam = lam_ref[...]
    e = jnp.exp(-jnp.abs(lam))
    e1 = 1.0 + e
    d = e1 - 1.0
    log1p_e = jnp.where(d == 0.0, e, jnp.log(e1) * (e / jnp.where(d == 0.0, 1.0, d)))
    a = jnp.exp2(((LRU_C * LOG2_E) * (jnp.minimum(lam, 0.0) - log1p_e)) * r)
    z1 = 1.0 - a * a
    u = (z1 * lax.rsqrt(jnp.maximum(z1, F32_TINY))) * (ig * xc)

    a = a.reshape(ng, SUBLANES, w)
    u = u.reshape(ng, SUBLANES, w)
    for k in (1, 2, 4):
        if reverse:
            keep = rm < SUBLANES - k
            shift = SUBLANES - k
        else:
            keep = rm >= k
            shift = k
        a_s = jnp.where(keep, pltpu.roll(a, shift, axis=1), 1.0)
        u_s = jnp.where(keep, pltpu.roll(u, shift, axis=1), 0.0)
        u = u + a * u_s
        a = a * a_s
    a_scr[...] = a.reshape(tc, w)
    u_scr[...] = u.reshape(tc, w)

    edge = 0 if reverse else SUBLANES - 1

    def chain(i, c):
        g = (ng - 1 - i) if reverse else i
        cin_scr[pl.ds(g, 1), :] = c
        r = g * SUBLANES + edge
        return u_scr[pl.ds(r, 1), :] + a_scr[pl.ds(r, 1), :] * c

    c_scr[...] = lax.fori_loop(0, ng, chain, c_scr[...], unroll=8)

    def apply(g, carry):
        r0 = pl.multiple_of(g * SUBLANES, SUBLANES)
        h_ref[pl.ds(r0, SUBLANES), :] = (u_scr[pl.ds(r0, SUBLANES), :]
                                         + a_scr[pl.ds(r0, SUBLANES), :] * cin_scr[pl.ds(g, 1), :])
        return carry

    lax.fori_loop(0, ng, apply, 0, unroll=8)


def _lru_scan(u3, conv_w, conv_b, w_gates_bf, b_gates, lam, reverse):
    bsz, n, _ = u3.shape
    w = LRU_WIDTH
    tc = min(LRU_TIME_TILE, n)
    nt = n // tc
    nb8 = n // SUBLANES
    per = tc // SUBLANES

    def tmap(t):
        return (nt - 1 - t) if reverse else t

    in_specs = [
        pl.BlockSpec((None, tc, w), lambda b, t: (b, tmap(t), COL_LRU_X)),
        pl.BlockSpec((None, SUBLANES, w), lambda b, t: (b, jnp.maximum(tmap(t) * per - 1, 0), COL_LRU_X)),
        pl.BlockSpec((None, SUBLANES, w), lambda b, t: (b, jnp.minimum((tmap(t) + 1) * per, nb8 - 1), COL_LRU_X)),
        pl.BlockSpec((CONV_W, w), lambda b, t: (0, 0)),
        pl.BlockSpec((1, w), lambda b, t: (0, 0)),
        pl.BlockSpec((w, 2 * w), lambda b, t: (0, 0)),
        pl.BlockSpec((1, 2 * w), lambda b, t: (0, 0)),
        pl.BlockSpec((1, w), lambda b, t: (0, 0)),
    ]
    out_spec = pl.BlockSpec((None, tc, w), lambda b, t: (b, tmap(t), 0))
    return pl.pallas_call(
        functools.partial(_lru_kernel, reverse=reverse, tc=tc),
        grid=(bsz, nt), in_specs=in_specs, out_specs=out_spec,
        out_shape=jax.ShapeDtypeStruct((bsz, n, w), F32),
        scratch_shapes=[pltpu.VMEM((tc, w), F32), pltpu.VMEM((tc, w), F32), pltpu.VMEM((tc // SUBLANES, w), F32),
                        pltpu.VMEM((1, w), F32)],
        compiler_params=_params(2), name="lru_bwd" if reverse else "lru_fwd",
    )(u3, u3, u3, conv_w, conv_b.reshape(1, w), w_gates_bf, b_gates.reshape(1, 2 * w), lam.reshape(1, w))


def _hgrn_kernel(q_ref, v_ref, f_ref, lb_ref, o_ref, st_scr, *, reverse, layer, tb, c):
    t = pl.program_id(1)

    @pl.when(t == 0)
    def _():
        st_scr[...] = jnp.zeros_like(st_scr)

    z = lb_ref[...]
    e = jnp.exp(z - jnp.max(z, axis=0, keepdims=True))
    p = e / jnp.sum(e, axis=0, keepdims=True)
    lb_all = jnp.sum(p[0:layer + 1], axis=0, keepdims=True) - p[0:1]

    ri = lax.broadcasted_iota(jnp.int32, (c, c), 0)
    ci = lax.broadcasted_iota(jnp.int32, (c, c), 1)
    valid = (ci >= ri) if reverse else (ci <= ri)
    tri = jnp.where(valid, 1.0, 0.0).astype(BF16)
    half = c // 2
    scale = HG_DK ** -0.5
    nchunks = tb // c
    nt_dims = (((1,), (1,)), ((), ()))
    tn_dims = (((0,), (0,)), ((), ()))

    order = [(j, h) for j in (range(nchunks - 1, -1, -1) if reverse else range(nchunks)) for h in range(HG_HEADS)]
    q_all, k_all, v_all, cum_all = {}, {}, {}, {}
    for j, h in order:
        sl = slice(j * c, (j + 1) * c)
        hl = slice(h * HG_DK, (h + 1) * HG_DK)
        lb = lb_all[:, hl]
        q_all[j, h] = _silu(q_ref[sl, hl]) * scale
        v_all[j, h] = v_ref[sl, hl].astype(BF16)
        f = lb + (1.0 - lb) * _sigmoid(f_ref[sl, hl])
        k_all[j, h] = 1.0 - f
        g = jnp.log2(f)
        g_hi = g.astype(BF16)
        r1 = g - g_hi.astype(F32)
        g_mid = r1.astype(BF16)
        g_lo = (r1 - g_mid.astype(F32)).astype(BF16)
        cum3 = jnp.dot(tri, jnp.concatenate([g_hi, g_mid, g_lo], axis=1), preferred_element_type=F32)
        cum_all[j, h] = (cum3[:, 0:HG_DK] + cum3[:, HG_DK:2 * HG_DK]) + cum3[:, 2 * HG_DK:3 * HG_DK]

    scores_all, qs_all, ds_all, dec_all = {}, {}, {}, {}
    for j, h in order:
        q, k, cum = q_all[j, h], k_all[j, h], cum_all[j, h]
        if reverse:
            mid = cum[half:half + 1]
            tot = cum[0:1]
        else:
            mid = cum[half - 1:half]
            tot = cum[c - 1:c]
        qt = q * jnp.exp2(jnp.minimum(cum - mid, HG_EXP2_CLAMP))
        kt = k * jnp.exp2(jnp.minimum(mid - cum, HG_EXP2_CLAMP))
        scores = lax.dot_general(qt.astype(BF16), kt.astype(BF16), nt_dims, preferred_element_type=F32)
        scores_all[j, h] = jnp.where(valid, scores, 0.0).astype(BF16)
        qs_all[j, h] = (q * jnp.exp2(cum)).astype(BF16)
        ks = (k * jnp.exp2(tot - cum)).astype(BF16)
        ds_all[j, h] = lax.dot_general(v_all[j, h], ks, tn_dims, preferred_element_type=F32)
        dec_all[j, h] = jnp.exp2(tot)

    intra_all = {}
    for j, h in order:
        intra_all[j, h] = jnp.dot(scores_all[j, h], v_all[j, h], preferred_element_type=F32)

    states = [st_scr[h] for h in range(HG_HEADS)]
    for j, h in order:
        st = states[h]
        o = intra_all[j, h] + lax.dot_general(qs_all[j, h], st.astype(BF16), nt_dims, preferred_element_type=F32)
        o_ref[j * c:(j + 1) * c, h * HG_DV:(h + 1) * HG_DV] = o
        states[h] = st * dec_all[j, h] + ds_all[j, h]
    for h in range(HG_HEADS):
        st_scr[h] = states[h]


def _hgrn_scan(u3, lb_raw, layer, reverse):
    bsz, n, _ = u3.shape
    depth = lb_raw.shape[0]
    tb = min(HG_TIME_TILE, n)
    nt = n // tb
    c = min(HG_CHUNK, tb)
    assert n % tb == 0 and tb % c == 0
    f_col = COL_F_BWD if reverse else COL_F_FWD

    def tmap(t):
        return (nt - 1 - t) if reverse else t

    in_specs = [
        pl.BlockSpec((None, tb, HG_W), lambda b, t: (b, tmap(t), COL_Q)),
        pl.BlockSpec((None, tb, HG_W), lambda b, t: (b, tmap(t), COL_I)),
        pl.BlockSpec((None, tb, HG_W), lambda b, t: (b, tmap(t), f_col)),
        pl.BlockSpec((depth, HG_W), lambda b, t: (0, 0)),
    ]
    out_spec = pl.BlockSpec((None, tb, HG_W), lambda b, t: (b, tmap(t), 0))
    return pl.pallas_call(
        functools.partial(_hgrn_kernel, reverse=reverse, layer=layer, tb=tb, c=c),
        grid=(bsz, nt), in_specs=in_specs, out_specs=out_spec,
        out_shape=jax.ShapeDtypeStruct((bsz, n, HG_W), F32),
        scratch_shapes=[pltpu.VMEM((HG_HEADS, HG_DV, HG_DK), F32)],
        compiler_params=_params(2), name="hgrn_bwd" if reverse else "hgrn_fwd",
    )(u3, u3, u3, lb_raw)


def _route(logits_t, b_router):
    mx = jnp.max(logits_t, axis=0, keepdims=True)
    ex = jnp.exp(logits_t - mx)
    probs = ex / jnp.sum(ex, axis=0, keepdims=True)
    sel = probs + b_router
    rows = [sel[i:i + 1] for i in range(N_EXPERTS)]
    prow = [probs[i:i + 1] for i in range(N_EXPERTS)]
    gscore = []
    for gi in range(N_GROUPS):
        m = rows[gi * EXPERTS_PER_GROUP:(gi + 1) * EXPERTS_PER_GROUP]
        best = None
        for a in range(EXPERTS_PER_GROUP):
            for b in range(a + 1, EXPERTS_PER_GROUP):
                s = m[a] + m[b]
                best = s if best is None else jnp.maximum(best, s)
        gscore.append(best)
    gbest = jnp.zeros_like(gscore[0], dtype=jnp.int32)
    gmax = gscore[0]
    for gi in range(1, N_GROUPS):
        better = gscore[gi] > gmax
        gbest = jnp.where(better, gi, gbest)
        gmax = jnp.where(better, gscore[gi], gmax)
    ms, mp = [], []
    for j in range(EXPERTS_PER_GROUP):
        s = rows[j]
        pr = prow[j]
        for gi in range(1, N_GROUPS):
            pick = gbest == gi
            s = jnp.where(pick, rows[gi * EXPERTS_PER_GROUP + j], s)
            pr = jnp.where(pick, prow[gi * EXPERTS_PER_GROUP + j], pr)
        ms.append(s)
        mp.append(pr)
    i1 = jnp.zeros_like(gbest)
    v1 = ms[0]
    for j in range(1, EXPERTS_PER_GROUP):
        better = ms[j] > v1
        i1 = jnp.where(better, j, i1)
        v1 = jnp.where(better, ms[j], v1)
    i2 = jnp.full_like(gbest, -1)
    v2 = jnp.full_like(v1, -jnp.inf)
    for j in range(EXPERTS_PER_GROUP):
        better = (i1 != j) & ((ms[j] > v2) | (i2 < 0))
        i2 = jnp.where(better, j, i2)
        v2 = jnp.where(better, ms[j], v2)
    p1 = mp[0]
    p2 = mp[0]
    for j in range(1, EXPERTS_PER_GROUP):
        p1 = jnp.where(i1 == j, mp[j], p1)
        p2 = jnp.where(i2 == j, mp[j], p2)
    den = p1 + p2
    w1 = p1 / den
    w2 = p2 / den
    first_low = i1 < i2
    lo = jnp.where(first_low, i1, i2)
    hi = jnp.where(first_low, i2, i1)
    pair = jnp.where(lo == 0, hi - 1, jnp.where(lo == 1, hi + 1, 5))
    cls = gbest * N_PAIRS + pair
    return cls, jnp.where(first_low, w1, w2), jnp.where(first_low, w2, w1)


def _mix_kernel(x_ref, hf_ref, hb_ref, ug_ref, of_ref, ob_ref, uhg_ref, ng_ref, wo_ref, g1_ref, b1_ref,
                wr_ref, br_ref, x1w_ref, cls_ref, rank_ref, cnt_ref, carry_scr, *, alpha):
    tm = x_ref.shape[0]

    @pl.when(pl.program_id(0) == 0)
    def _():
        carry_scr[...] = jnp.zeros_like(carry_scr)

    lru = (hf_ref[...] + hb_ref[...]) * jax.nn.gelu(ug_ref[...], approximate=True)
    o = of_ref[...] + ob_ref[...]
    heads = []
    for h in range(HG_HEADS):
        oh = o[:, h * HG_DV:(h + 1) * HG_DV]
        heads.append(oh * lax.rsqrt(jnp.mean(oh * oh, axis=-1, keepdims=True) + RMS_EPS))
    hg = jnp.concatenate(heads, axis=1) * ng_ref[...] * _silu(uhg_ref[...])
    mix_in = jnp.concatenate([lru, hg], axis=1).astype(BF16)
    mix = jnp.dot(mix_in, wo_ref[...], preferred_element_type=F32)
    x1 = _layer_norm(alpha * x_ref[...] + mix, g1_ref[...], b1_ref[...])
    def split(v):
        hi = v.astype(BF16)
        return hi, (v - hi.astype(F32)).astype(BF16)

    def dot_nt(a, b):
        return lax.dot_general(a, b, (((1,), (1,)), ((), ())), preferred_element_type=F32)

    wr_hi, wr_lo = split(wr_ref[...])
    x_hi, x_lo = split(x1)
    logits_t = dot_nt(wr_hi, x_hi) + (dot_nt(wr_hi, x_lo) + dot_nt(wr_lo, x_hi))
    cls, w_lo, w_hi = _route(logits_t, br_ref[...])
    rid = lax.broadcasted_iota(jnp.int32, (LANES, tm), 0)
    route_t = jnp.where(rid == 0, w_lo, jnp.where(rid == 1, w_hi, 0.0))
    x1w_ref[...] = jnp.concatenate([x1, route_t.T], axis=1)
    cid = lax.broadcasted_iota(jnp.int32, (CLASS_ROWS, tm), 0)
    onehot = jnp.where(cid == cls, 1.0, 0.0)
    si = lax.broadcasted_iota(jnp.int32, (tm, tm), 0)
    ti = lax.broadcasted_iota(jnp.int32, (tm, tm), 1)
    upper = jnp.where(si <= ti, 1.0, 0.0).astype(BF16)
    prefix = jnp.dot(onehot.astype(BF16), upper, preferred_element_type=F32)
    carry = carry_scr[...]
    rank = jnp.sum(onehot * (prefix - 1.0 + carry), axis=0, keepdims=True)
    cls_ref[...] = cls
    rank_ref[...] = rank.astype(jnp.int32)
    carry = carry + prefix[:, tm - 1:tm]
    carry_scr[...] = carry
    cnt_ref[...] = jnp.broadcast_to(carry, cnt_ref.shape)


def _mix(x2d, hf, hb, u2d, of, ob, norm_g, w_out_bf, ln_g, ln_b, w_router_t, b_router, alpha):
    t, d = x2d.shape
    w = LRU_WIDTH
    tm = min(TOKEN_TILE, t)
    assert t % tm == 0
    nt = t // tm
    row = lambda i: (i, 0)
    const = lambda i: (0, 0)
    in_specs = [
        pl.BlockSpec((tm, d), row),
        pl.BlockSpec((tm, w), row),
        pl.BlockSpec((tm, w), row),
        pl.BlockSpec((tm, w), lambda i: (i, COL_LRU_GATE)),
        pl.BlockSpec((tm, w), row),
        pl.BlockSpec((tm, w), row),
        pl.BlockSpec((tm, w), lambda i: (i, COL_G)),
        pl.BlockSpec((1, w), const),
        pl.BlockSpec((2 * w, d), const),
        pl.BlockSpec((1, d), const),
        pl.BlockSpec((1, d), const),
        pl.BlockSpec((N_EXPERTS, d), const),
        pl.BlockSpec((N_EXPERTS, 1), const),
    ]
    tok_spec = pl.BlockSpec((None, 1, tm), lambda i: (i, 0, 0))
    out_specs = [pl.BlockSpec((tm, d + LANES), row), tok_spec, tok_spec, pl.BlockSpec((CLASS_ROWS, LANES), const)]
    x1w, cls, rank, cnt = pl.pallas_call(
        functools.partial(_mix_kernel, alpha=alpha),
        grid=(nt,), in_specs=in_specs, out_specs=out_specs,
        out_shape=[jax.ShapeDtypeStruct((t, d + LANES), F32), jax.ShapeDtypeStruct((nt, 1, tm), jnp.int32),
                   jax.ShapeDtypeStruct((nt, 1, tm), jnp.int32), jax.ShapeDtypeStruct((CLASS_ROWS, LANES), F32)],
        scratch_shapes=[pltpu.VMEM((CLASS_ROWS, 1), F32)],
        compiler_params=_params(1), name="mix_ln_route",
    )(x2d, hf, hb, u2d, of, ob, u2d, norm_g.reshape(1, w), w_out_bf, ln_g.reshape(1, d), ln_b.reshape(1, d),
      w_router_t, b_router.reshape(N_EXPERTS, 1))
    return x1w, cls.reshape(t), rank.reshape(t), cnt[:N_CLASSES, 0].astype(jnp.int32)


def _route_plan(cls, rank, counts, t, tm):
    nt = t // tm + N_CLASSES
    tiles_c = (counts + tm - 1) // tm
    tile_end = jnp.cumsum(tiles_c)
    pos = ((tile_end - tiles_c) * tm)[cls] + rank
    last_tile = jnp.where(tiles_c > 0, tile_end - 1, -1).astype(jnp.int32)
    used = tile_end[-1:].astype(jnp.int32)
    xt = jnp.minimum(jnp.arange(nt, dtype=jnp.int32), tile_end[-1] - 1)
    tile_cls = jnp.sum((tile_end[None, :] <= xt[:, None]).astype(jnp.int32), axis=1)
    group = tile_cls // N_PAIRS
    pair = tile_cls % N_PAIRS
    ea = group * EXPERTS_PER_GROUP + jnp.array(PAIR_LO, jnp.int32)[pair]
    eb = group * EXPERTS_PER_GROUP + jnp.array(PAIR_HI, jnp.int32)[pair]
    return pos, last_tile, used, xt, ea, eb


def _dispatch_kernel(last_ref, used_ref, pos_ref, x_ref, xs_hbm, zbuf, xbuf, sem_z, sem, *, td, tm, n_tiles):
    i = pl.program_id(0)
    slot = lax.rem(i, 2)

    def zero_tile(j):
        cp = pltpu.make_async_copy(zbuf, xs_hbm.at[pl.ds(pl.multiple_of(j * tm, tm), tm)], sem_z.at[0])
        cp.start()
        cp.wait()

    @pl.when(i == 0)
    def _():
        zbuf[...] = jnp.zeros_like(zbuf)
        for c in range(N_CLASSES):
            @pl.when(last_ref[c] >= 0)
            def _():
                zero_tile(last_ref[c])

        def tail(j, carry):
            zero_tile(j)
            return carry
        lax.fori_loop(used_ref[0], n_tiles, tail, 0)

    def wait_slot(s):
        pltpu.make_async_copy(xbuf.at[s], xs_hbm.at[pl.ds(0, td)], sem.at[s]).wait()

    @pl.when(i >= 2)
    def _():
        wait_slot(slot)

    xbuf[slot] = x_ref[...]
    for r in range(td):
        pltpu.make_async_copy(xbuf.at[slot, pl.ds(r, 1)], xs_hbm.at[pl.ds(pos_ref[0, r], 1)],
                              sem.at[slot]).start(priority=r % 2)

    @pl.when(i == pl.num_programs(0) - 1)
    def _():
        wait_slot(slot)

        @pl.when(i >= 1)
        def _():
            wait_slot(1 - slot)


def _dispatch(x1w, pos, last_tile, used, tm, n_tiles):
    t, w = x1w.shape
    nt, _, td = pos.shape
    grid_spec = pltpu.PrefetchScalarGridSpec(
        num_scalar_prefetch=2, grid=(nt,),
        in_specs=[pl.BlockSpec((None, 1, td), lambda i, lt, us: (i, 0, 0), memory_space=pltpu.SMEM),
                  pl.BlockSpec((td, w), lambda i, lt, us: (i, 0))],
        out_specs=pl.BlockSpec(memory_space=pl.ANY),
        scratch_shapes=[pltpu.VMEM((tm, w), F32), pltpu.VMEM((2, td, w), F32),
                        pltpu.SemaphoreType.DMA((1,)), pltpu.SemaphoreType.DMA((2,))])
    return pl.pallas_call(
        functools.partial(_dispatch_kernel, td=td, tm=tm, n_tiles=n_tiles), grid_spec=grid_spec,
        out_shape=jax.ShapeDtypeStruct((n_tiles * tm, w), F32),
        compiler_params=_params(1), name="dispatch",
    )(last_tile, used, pos, x1w)


def _start_row_gather(idx_ref, src_hbm, buf, sem, slot, rows):
    for r in range(rows):
        pltpu.make_async_copy(src_hbm.at[pl.ds(idx_ref[0, r], 1)], buf.at[slot, pl.ds(r, 1)],
                              sem.at[slot]).start(priority=r % 2)


def _wait_row_gather(src_hbm, buf, sem, slot, rows):
    pltpu.make_async_copy(src_hbm.at[pl.ds(0, rows)], buf.at[slot], sem.at[slot]).wait()


def _moe_kernel(xt_ref, ea_ref, eb_ref, x_ref, wga_ref, wua_ref, wda_ref, wgb_ref, wub_ref, wdb_ref, y_ref, *, d):
    del xt_ref, ea_ref, eb_ref
    xw = x_ref[...]
    x = xw[:, :d].astype(BF16)
    w_lo = xw[:, d:d + 1]
    w_hi = xw[:, d + 1:d + 2]

    def expert(wg_ref, wu_ref, wd_ref):
        hid = _silu(jnp.dot(x, wg_ref[...], preferred_element_type=F32)) * jnp.dot(
            x, wu_ref[...], preferred_element_type=F32)
        return jnp.dot(hid.astype(BF16), wd_ref[...], preferred_element_type=F32)

    y_ref[...] = w_lo * expert(wga_ref, wua_ref, wda_ref) + w_hi * expert(wgb_ref, wub_ref, wdb_ref)


def _moe(xs, xt, ea, eb, tm, wg_bf, wu_bf, wd_bf):
    d = xs.shape[1] - LANES
    _, _, dff = wg_bf.shape
    nt = xs.shape[0] // tm
    up_a = pl.BlockSpec((None, d, dff), lambda i, xt, ea, eb: (ea[i], 0, 0))
    down_a = pl.BlockSpec((None, dff, d), lambda i, xt, ea, eb: (ea[i], 0, 0))
    up_b = pl.BlockSpec((None, d, dff), lambda i, xt, ea, eb: (eb[i], 0, 0))
    down_b = pl.BlockSpec((None, dff, d), lambda i, xt, ea, eb: (eb[i], 0, 0))
    grid_spec = pltpu.PrefetchScalarGridSpec(
        num_scalar_prefetch=3, grid=(nt,),
        in_specs=[pl.BlockSpec((tm, d + LANES), lambda i, xt, ea, eb: (xt[i], 0)),
                  up_a, up_a, down_a, up_b, up_b, down_b],
        out_specs=pl.BlockSpec((tm, d), lambda i, xt, ea, eb: (i, 0)))
    return pl.pallas_call(
        functools.partial(_moe_kernel, d=d), grid_spec=grid_spec,
        out_shape=jax.ShapeDtypeStruct((nt * tm, d), F32),
        compiler_params=_params(1), name="moe_routed",
    )(xt, ea, eb, xs, wg_bf, wu_bf, wd_bf, wg_bf, wu_bf, wd_bf)


def _ln2_gathered(pos_ref, posn_ref, x1_ref, ys_hbm, g_ref, b_ref, ybuf, sem, alpha):
    i = pl.program_id(0)
    tm = x1_ref.shape[0]
    slot = lax.rem(i, 2)

    @pl.when(i == 0)
    def _():
        _start_row_gather(pos_ref, ys_hbm, ybuf, sem, 0, tm)

    _wait_row_gather(ys_hbm, ybuf, sem, slot, tm)
    _start_row_gather(posn_ref, ys_hbm, ybuf, sem, 1 - slot, tm)
    return _layer_norm(alpha * x1_ref[...] + ybuf[slot], g_ref[...], b_ref[...])


def _ln2_finish(ys_hbm, ybuf, sem):
    i = pl.program_id(0)

    @pl.when(i == pl.num_programs(0) - 1)
    def _():
        _wait_row_gather(ys_hbm, ybuf, sem, 1 - lax.rem(i, 2), ybuf.shape[1])


def _ln2_in_proj_kernel(pos_ref, posn_ref, x1_ref, ys_hbm, g_ref, b_ref, w_ref, u_ref, x2_ref, ybuf, sem, *, alpha):
    x2 = _ln2_gathered(pos_ref, posn_ref, x1_ref, ys_hbm, g_ref, b_ref, ybuf, sem, alpha)
    x2_ref[...] = x2
    u_ref[...] = jnp.dot(x2.astype(BF16), w_ref[...], preferred_element_type=F32)
    _ln2_finish(ys_hbm, ybuf, sem)


def _ln2_kernel(pos_ref, posn_ref, x1_ref, ys_hbm, g_ref, b_ref, x2_ref, ybuf, sem, *, alpha):
    x2_ref[...] = _ln2_gathered(pos_ref, posn_ref, x1_ref, ys_hbm, g_ref, b_ref, ybuf, sem, alpha)
    _ln2_finish(ys_hbm, ybuf, sem)


def _ln2(x1w, ys, pos, ln_g, ln_b, alpha, w_in_bf=None):
    t = x1w.shape[0]
    d = ys.shape[1]
    nt, _, tm = pos.shape
    row = lambda i: (i, 0)
    const = lambda i: (0, 0)
    x_spec = pl.BlockSpec((tm, d), row)
    vec_spec = pl.BlockSpec((1, d), const)
    in_specs = [pl.BlockSpec((None, 1, tm), lambda i: (i, 0, 0), memory_space=pltpu.SMEM),
                pl.BlockSpec((None, 1, tm), lambda i: (jnp.minimum(i + 1, nt - 1), 0, 0), memory_space=pltpu.SMEM),
                x_spec, pl.BlockSpec(memory_space=pl.ANY), vec_spec, vec_spec]
    scratch = [pltpu.VMEM((2, tm, d), F32), pltpu.SemaphoreType.DMA((2,))]
    args = (pos, pos, x1w, ys, ln_g.reshape(1, d), ln_b.reshape(1, d))
    if w_in_bf is None:
        return pl.pallas_call(
            functools.partial(_ln2_kernel, alpha=alpha), grid=(nt,), in_specs=in_specs, out_specs=x_spec,
            out_shape=jax.ShapeDtypeStruct((t, d), F32), scratch_shapes=scratch,
            compiler_params=_params(1), name="ln2")(*args)
    cols = w_in_bf.shape[1]
    return pl.pallas_call(
        functools.partial(_ln2_in_proj_kernel, alpha=alpha), grid=(nt,),
        in_specs=in_specs + [pl.BlockSpec((d, cols), const)],
        out_specs=[pl.BlockSpec((tm, cols), row), x_spec],
        out_shape=[jax.ShapeDtypeStruct((t, cols), F32), jax.ShapeDtypeStruct((t, d), F32)],
        scratch_shapes=scratch, compiler_params=_params(1), name="ln2_in_proj")(*args, w_in_bf)


def _block_diag(blocks):
    nb, bw, _ = blocks.shape
    eye = jnp.eye(nb, dtype=blocks.dtype)
    return jnp.einsum("hij,hg->higj", blocks, eye).reshape(nb * bw, nb * bw)


def _trunk(x, p):
    bsz, n, d = x.shape
    depth = p["w_in"].shape[0]
    alpha = (2 * depth) ** 0.25
    t = bsz * n
    u2d, x2d = _in_proj(x.reshape(t, d), p["w_in_bf"][0], (p["ln_in_g"], p["ln_in_b"]))
    for l in range(depth):
        u3 = u2d.reshape(bsz, n, -1)
        hs, os_ = [], []
        for di, rev in enumerate((False, True)):
            hs.append(_lru_scan(u3, p["conv_w"][l], p["conv_b"][l], p["w_gates_bf"][l][di], p["b_gates"][l][di],
                                p["lru_lambda"][l][di], rev))
            os_.append(_hgrn_scan(u3, p["hgrn_lb"][di], l, rev))
        x1w, cls, rank, counts = _mix(
            x2d, hs[0].reshape(t, -1), hs[1].reshape(t, -1), u2d, os_[0].reshape(t, -1), os_[1].reshape(t, -1),
            p["hgrn_norm_g"][l], p["w_out_bf"][l], p["ln1_g"][l], p["ln1_b"][l], p["w_router_t"], p["b_router"], alpha)
        tile_rows = min(MOE_ROW_TILE, t)
        pos, last_tile, used, xt, ea, eb = _route_plan(cls, rank, counts, t, tile_rows)
        tm = min(TOKEN_TILE, t)
        pos = pos.reshape(t // tm, 1, tm)
        xs = _dispatch(x1w, pos, last_tile, used, tile_rows, xt.shape[0])
        ys = _moe(xs, xt, ea, eb, tile_rows, p["w_gate_bf"][l], p["w_up_bf"][l], p["w_down_bf"][l])
        if l + 1 < depth:
            u2d, x2d = _ln2(x1w, ys, pos, p["ln2_g"][l], p["ln2_b"][l], alpha, p["w_in_bf"][l + 1])
        else:
            x2d = _ln2(x1w, ys, pos, p["ln2_g"][l], p["ln2_b"][l], alpha)
    return x2d.reshape(bsz, n, d)


def kernel(x_prompt, x_sample, ln_in_g, ln_in_b, w_in, conv_w, conv_b, lru_wa, lru_ba, lru_wx, lru_bx, lru_lambda,
           hgrn_lb, hgrn_norm_g, w_out, ln1_g, ln1_b, w_router, b_router, w_gate, w_up, w_down, ln2_g, ln2_b):
    depth = w_in.shape[0]
    d = w_in.shape[1]
    w_gates = jnp.stack([
        jnp.stack([jnp.concatenate([_block_diag(lru_wa[l, di]), _block_diag(lru_wx[l, di])], axis=1)
                   for di in range(2)]) for l in range(depth)])
    b_gates = jnp.concatenate([lru_ba, lru_bx], axis=-1)
    w_router_t = w_router.T
    p = dict(
        ln_in_g=ln_in_g, ln_in_b=ln_in_b, w_in_bf=w_in.astype(BF16), conv_w=conv_w, conv_b=conv_b,
        w_gates_bf=w_gates.astype(BF16), b_gates=b_gates, lru_lambda=lru_lambda, hgrn_lb=hgrn_lb,
        hgrn_norm_g=hgrn_norm_g, w_out_bf=w_out.astype(BF16), ln1_g=ln1_g, ln1_b=ln1_b,
        w_router_t=w_router_t, b_router=b_router, w_gate_bf=w_gate.astype(BF16), w_up_bf=w_up.astype(BF16),
        w_down_bf=w_down.astype(BF16), ln2_g=ln2_g, ln2_b=ln2_b, w_in=w_in)
    return (_trunk(x_prompt, p), _trunk(x_sample, p))
```

```python
import functools

import jax
import jax.numpy as jnp
from jax import lax
from jax.experimental import pallas as pl
from jax.experimental.pallas import tpu as pltpu

F32 = jnp.float32
BF16 = jnp.bfloat16

LRU_WIDTH = 512
LRU_BLOCKS = 8
CONV_W = 4
CONV_LEFT = 2
LRU_C = 8.0
HG_HEADS = 4
HG_DK = 128
HG_DV = 128
HG_W = HG_HEADS * HG_DK
N_EXPERTS = 16
N_GROUPS = 4
EXPERTS_PER_GROUP = N_EXPERTS // N_GROUPS
PAIR_LO = (0, 0, 0, 1, 1, 2)
PAIR_HI = (1, 2, 3, 2, 3, 3)
N_PAIRS = len(PAIR_LO)
N_CLASSES = N_GROUPS * N_PAIRS
CLASS_ROWS = 32
LN_EPS = 1e-5
RMS_EPS = 1e-6
F32_TINY = 1.1754944e-38
LOG2_E = 1.4426950408889634
COL_LRU_X, COL_LRU_GATE, COL_Q, COL_I, COL_F_FWD, COL_F_BWD, COL_G = range(7)

LANES = 128
SUBLANES = 8
VMEM_LIMIT_BYTES = 56 * 1024 * 1024

TOKEN_TILE = 512
LRU_TIME_TILE = 1024
HG_TIME_TILE = 1024
HG_CHUNK = 64
MOE_ROW_TILE = 256
HG_EXP2_CLAMP = 115.0


def _params(n_axes):
    return pltpu.CompilerParams(dimension_semantics=("arbitrary",) * n_axes,
                                vmem_limit_bytes=VMEM_LIMIT_BYTES)


def _layer_norm(x, g, b):
    mu = jnp.mean(x, axis=-1, keepdims=True)
    xc = x - mu
    var = jnp.mean(xc * xc, axis=-1, keepdims=True)
    return xc * lax.rsqrt(var + LN_EPS) * g + b


def _sigmoid(x):
    return 0.5 + 0.5 * jnp.tanh(0.5 * x)


def _silu(x):
    return x * _sigmoid(x)


def _in_proj_ln_kernel(x_ref, g_ref, b_ref, w_ref, u_ref, xn_ref):
    xn = _layer_norm(x_ref[...], g_ref[...], b_ref[...])
    xn_ref[...] = xn
    u_ref[...] = jnp.dot(xn.astype(BF16), w_ref[...], preferred_element_type=F32)


def _in_proj_kernel(x_ref, w_ref, u_ref):
    u_ref[...] = jnp.dot(x_ref[...].astype(BF16), w_ref[...], preferred_element_type=F32)


def _in_proj(x2d, w_in_bf, ln=None):
    t, d = x2d.shape
    cols = w_in_bf.shape[1]
    tm = min(TOKEN_TILE, t)
    grid = (t // tm,)
    x_spec = pl.BlockSpec((tm, d), lambda i: (i, 0))
    w_spec = pl.BlockSpec((d, cols), lambda i: (0, 0))
    u_spec = pl.BlockSpec((tm, cols), lambda i: (i, 0))
    vec_spec = pl.BlockSpec((1, d), lambda i: (0, 0))
    if ln is None:
        u = pl.pallas_call(
            _in_proj_kernel, grid=grid, in_specs=[x_spec, w_spec], out_specs=u_spec,
            out_shape=jax.ShapeDtypeStruct((t, cols), F32), compiler_params=_params(1),
            name="in_proj")(x2d, w_in_bf)
        return u, x2d
    g, b = ln
    u, xn = pl.pallas_call(
        _in_proj_ln_kernel, grid=grid, in_specs=[x_spec, vec_spec, vec_spec, w_spec],
        out_specs=[u_spec, x_spec],
        out_shape=[jax.ShapeDtypeStruct((t, cols), F32), jax.ShapeDtypeStruct((t, d), F32)],
        compiler_params=_params(1), name="in_proj_ln")(x2d, g.reshape(1, d), b.reshape(1, d), w_in_bf)
    return u, xn


def _lru_kernel(cur_ref, prev_ref, next_ref, cw_ref, cb_ref, wg_ref, bg_ref, lam_ref, h_ref,
                a_scr, u_scr, cin_scr, c_scr, *, reverse, tc):
    t = pl.program_id(1)
    nt = pl.num_programs(1)
    tt = (nt - 1 - t) if reverse else t
    w = LRU_WIDTH
    ng = tc // SUBLANES

    @pl.when(t == 0)
    def _():
        c_scr[...] = jnp.zeros_like(c_scr)

    cur = cur_ref[...].reshape(ng, SUBLANES, w)
    prev = jnp.where(tt > 0, prev_ref[...], 0.0)
    nxt = jnp.where(tt < nt - 1, next_ref[...], 0.0)
    rm = lax.broadcasted_iota(jnp.int32, (ng, SUBLANES, w), 1)

    def shift_down(k):
        s = pltpu.roll(cur, k, axis=1)
        before = jnp.concatenate([pltpu.roll(prev, k, axis=0)[None], s[:ng - 1]], axis=0)
        return jnp.where(rm < k, before, s)

    s_up = pltpu.roll(cur, SUBLANES - 1, axis=1)
    after = jnp.concatenate([s_up[1:], pltpu.roll(nxt, SUBLANES - 1, axis=0)[None]], axis=0)
    x_p1 = jnp.where(rm == SUBLANES - 1, after, s_up)

    cw = cw_ref[...]
    xc = shift_down(2) * cw[0:1] + shift_down(1) * cw[1:2]
    xc = xc + cur * cw[2:3]
    xc = xc + x_p1 * cw[3:4]
    xc = (xc + cb_ref[...]).reshape(tc, w)

    z = jnp.dot(xc.astype(BF16), wg_ref[...], preferred_element_type=F32) + bg_ref[...]
    r = _sigmoid(z[:, :w])
    ig = _sigmoid(z[:, w:])
    lam = lam_ref[...]
    e = jnp.exp(-jnp.abs(lam))
    e1 = 1.0 + e
    d = e1 - 1.0
    log1p_e = jnp.where(d == 0.0, e, jnp.log(e1) * (e / jnp.where(d == 0.0, 1.0, d)))
    a = jnp.exp2(((LRU_C * LOG2_E) * (jnp.minimum(lam, 0.0) - log1p_e)) * r)
    z1 = 1.0 - a * a
    u = (z1 * lax.rsqrt(jnp.maximum(z1, F32_TINY))) * (ig * xc)

    a = a.reshape(ng, SUBLANES, w)
    u = u.reshape(ng, SUBLANES, w)
    for k in (1, 2, 4):
        if reverse:
            keep = rm < SUBLANES - k
            shift = SUBLANES - k
        else:
            keep = rm >= k
            shift = k
        a_s = jnp.where(keep, pltpu.roll(a, shift, axis=1), 1.0)
        u_s = jnp.where(keep, pltpu.roll(u, shift, axis=1), 0.0)
        u = u + a * u_s
        a = a * a_s
    a_scr[...] = a.reshape(tc, w)
    u_scr[...] = u.reshape(tc, w)

    edge = 0 if reverse else SUBLANES - 1

    def chain(i, c):
        g = (ng - 1 - i) if reverse else i
        cin_scr[pl.ds(g, 1), :] = c
        r = g * SUBLANES + edge
        return u_scr[pl.ds(r, 1), :] + a_scr[pl.ds(r, 1), :] * c

    c_scr[...] = lax.fori_loop(0, ng, chain, c_scr[...], unroll=8)

    def apply(g, carry):
        r0 = pl.multiple_of(g * SUBLANES, SUBLANES)
        h_ref[pl.ds(r0, SUBLANES), :] = (u_scr[pl.ds(r0, SUBLANES), :]
                                         + a_scr[pl.ds(r0, SUBLANES), :] * cin_scr[pl.ds(g, 1), :])
        return carry

    lax.fori_loop(0, ng, apply, 0, unroll=8)


def _lru_scan(u3, conv_w, conv_b, w_gates_bf, b_gates, lam, reverse):
    bsz, n, _ = u3.shape
    w = LRU_WIDTH
    tc = min(LRU_TIME_TILE, n)
    nt = n // tc
    nb8 = n // SUBLANES
    per = tc // SUBLANES

    def tmap(t):
        return (nt - 1 - t) if reverse else t

    in_specs = [
        pl.BlockSpec((None, tc, w), lambda b, t: (b, tmap(t), COL_LRU_X)),
        pl.BlockSpec((None, SUBLANES, w), lambda b, t: (b, jnp.maximum(tmap(t) * per - 1, 0), COL_LRU_X)),
        pl.BlockSpec((None, SUBLANES, w), lambda b, t: (b, jnp.minimum((tmap(t) + 1) * per, nb8 - 1), COL_LRU_X)),
        pl.BlockSpec((CONV_W, w), lambda b, t: (0, 0)),
        pl.BlockSpec((1, w), lambda b, t: (0, 0)),
        pl.BlockSpec((w, 2 * w), lambda b, t: (0, 0)),
        pl.BlockSpec((1, 2 * w), lambda b, t: (0, 0)),
        pl.BlockSpec((1, w), lambda b, t: (0, 0)),
    ]
    out_spec = pl.BlockSpec((None, tc, w), lambda b, t: (b, tmap(t), 0))
    return pl.pallas_call(
        functools.partial(_lru_kernel, reverse=reverse, tc=tc),
        grid=(bsz, nt), in_specs=in_specs, out_specs=out_spec,
        out_shape=jax.ShapeDtypeStruct((bsz, n, w), F32),
        scratch_shapes=[pltpu.VMEM((tc, w), F32), pltpu.VMEM((tc, w), F32), pltpu.VMEM((tc // SUBLANES, w), F32),
                        pltpu.VMEM((1, w), F32)],
        compiler_params=_params(2), name="lru_bwd" if reverse else "lru_fwd",
    )(u3, u3, u3, conv_w, conv_b.reshape(1, w), w_gates_bf, b_gates.reshape(1, 2 * w), lam.reshape(1, w))


def _hgrn_kernel(q_ref, v_ref, f_ref, lb_ref, o_ref, st_scr, *, reverse, layer, tb, c):
    t = pl.program_id(1)

    @pl.when(t == 0)
    def _():
        st_scr[...] = jnp.zeros_like(st_scr)

    z = lb_ref[...]
    e = jnp.exp(z - jnp.max(z, axis=0, keepdims=True))
    p = e / jnp.sum(e, axis=0, keepdims=True)
    lb_all = jnp.sum(p[0:layer + 1], axis=0, keepdims=True) - p[0:1]

    ri = lax.broadcasted_iota(jnp.int32, (c, c), 0)
    ci = lax.broadcasted_iota(jnp.int32, (c, c), 1)
    valid = (ci >= ri) if reverse else (ci <= ri)
    tri = jnp.where(valid, 1.0, 0.0).astype(BF16)
    half = c // 2
    scale = HG_DK ** -0.5
    nchunks = tb // c
    nt_dims = (((1,), (1,)), ((), ()))
    tn_dims = (((0,), (0,)), ((), ()))

    order = [(j, h) for j in (range(nchunks - 1, -1, -1) if reverse else range(nchunks)) for h in range(HG_HEADS)]
    q_all, k_all, v_all, cum_all = {}, {}, {}, {}
    for j, h in order:
        sl = slice(j * c, (j + 1) * c)
        hl = slice(h * HG_DK, (h + 1) * HG_DK)
        lb = lb_all[:, hl]
        q_all[j, h] = _silu(q_ref[sl, hl]) * scale
        v_all[j, h] = v_ref[sl, hl].astype(BF16)
        f = lb + (1.0 - lb) * _sigmoid(f_ref[sl, hl])
        k_all[j, h] = 1.0 - f
        g = jnp.log2(f)
        g_hi = g.astype(BF16)
        r1 = g - g_hi.astype(F32)
        g_mid = r1.astype(BF16)
        g_lo = (r1 - g_mid.astype(F32)).astype(BF16)
        cum3 = jnp.dot(tri, jnp.concatenate([g_hi, g_mid, g_lo], axis=1), preferred_element_type=F32)
        cum_all[j, h] = (cum3[:, 0:HG_DK] + cum3[:, HG_DK:2 * HG_DK]) + cum3[:, 2 * HG_DK:3 * HG_DK]

    scores_all, qs_all, ds_all, dec_all = {}, {}, {}, {}
    for j, h in order:
        q, k, cum = q_all[j, h], k_all[j, h], cum_all[j, h]
        if reverse:
            mid = cum[half:half + 1]
            tot = cum[0:1]
        else:
            mid = cum[half - 1:half]
            tot = cum[c - 1:c]
        qt = q * jnp.exp2(jnp.minimum(cum - mid, HG_EXP2_CLAMP))
        kt = k * jnp.exp2(jnp.minimum(mid - cum, HG_EXP2_CLAMP))
        scores = lax.dot_general(qt.astype(BF16), kt.astype(BF16), nt_dims, preferred_element_type=F32)
        scores_all[j, h] = jnp.where(valid, scores, 0.0).astype(BF16)
        qs_all[j, h] = (q * jnp.exp2(cum)).astype(BF16)
        ks = (k * jnp.exp2(tot - cum)).astype(BF16)
        ds_all[j, h] = lax.dot_general(v_all[j, h], ks, tn_dims, preferred_element_type=F32)
        dec_all[j, h] = jnp.exp2(tot)

    intra_all = {}
    for j, h in order:
        intra_all[j, h] = jnp.dot(scores_all[j, h], v_all[j, h], preferred_element_type=F32)

    states = [st_scr[h] for h in range(HG_HEADS)]
    for j, h in order:
        st = states[h]
        o = intra_all[j, h] + lax.dot_general(qs_all[j, h], st.astype(BF16), nt_dims, preferred_element_type=F32)
        o_ref[j * c:(j + 1) * c, h * HG_DV:(h + 1) * HG_DV] = o
        states[h] = st * dec_all[j, h] + ds_all[j, h]
    for h in range(HG_HEADS):
        st_scr[h] = states[h]


def _hgrn_scan(u3, lb_raw, layer, reverse):
    bsz, n, _ = u3.shape
    depth = lb_raw.shape[0]
    tb = min(HG_TIME_TILE, n)
    nt = n // tb
    c = min(HG_CHUNK, tb)
    assert n % tb == 0 and tb % c == 0
    f_col = COL_F_BWD if reverse else COL_F_FWD

    def tmap(t):
        return (nt - 1 - t) if reverse else t

    in_specs = [
        pl.BlockSpec((None, tb, HG_W), lambda b, t: (b, tmap(t), COL_Q)),
        pl.BlockSpec((None, tb, HG_W), lambda b, t: (b, tmap(t), COL_I)),
        pl.BlockSpec((None, tb, HG_W), lambda b, t: (b, tmap(t), f_col)),
        pl.BlockSpec((depth, HG_W), lambda b, t: (0, 0)),
    ]
    out_spec = pl.BlockSpec((None, tb, HG_W), lambda b, t: (b, tmap(t), 0))
    return pl.pallas_call(
        functools.partial(_hgrn_kernel, reverse=reverse, layer=layer, tb=tb, c=c),
        grid=(bsz, nt), in_specs=in_specs, out_specs=out_spec,
        out_shape=jax.ShapeDtypeStruct((bsz, n, HG_W), F32),
        scratch_shapes=[pltpu.VMEM((HG_HEADS, HG_DV, HG_DK), F32)],
        compiler_params=_params(2), name="hgrn_bwd" if reverse else "hgrn_fwd",
    )(u3, u3, u3, lb_raw)


def _route(logits_t, b_router):
    mx = jnp.max(logits_t, axis=0, keepdims=True)
    ex = jnp.exp(logits_t - mx)
    probs = ex / jnp.sum(ex, axis=0, keepdims=True)
    sel = probs + b_router
    rows = [sel[i:i + 1] for i in range(N_EXPERTS)]
    prow = [probs[i:i + 1] for i in range(N_EXPERTS)]
    gscore = []
    for gi in range(N_GROUPS):
        m = rows[gi * EXPERTS_PER_GROUP:(gi + 1) * EXPERTS_PER_GROUP]
        best = None
        for a in range(EXPERTS_PER_GROUP):
            for b in range(a + 1, EXPERTS_PER_GROUP):
                s = m[a] + m[b]
                best = s if best is None else jnp.maximum(best, s)
        gscore.append(best)
    gbest = jnp.zeros_like(gscore[0], dtype=jnp.int32)
    gmax = gscore[0]
    for gi in range(1, N_GROUPS):
        better = gscore[gi] > gmax
        gbest = jnp.where(better, gi, gbest)
        gmax = jnp.where(better, gscore[gi], gmax)
    ms, mp = [], []
    for j in range(EXPERTS_PER_GROUP):
        s = rows[j]
        pr = prow[j]
        for gi in range(1, N_GROUPS):
            pick = gbest == gi
            s = jnp.where(pick, rows[gi * EXPERTS_PER_GROUP + j], s)
            pr = jnp.where(pick, prow[gi * EXPERTS_PER_GROUP + j], pr)
        ms.append(s)
        mp.append(pr)
    i1 = jnp.zeros_like(gbest)
    v1 = ms[0]
    for j in range(1, EXPERTS_PER_GROUP):
        better = ms[j] > v1
        i1 = jnp.where(better, j, i1)
        v1 = jnp.where(better, ms[j], v1)
    i2 = jnp.full_like(gbest, -1)
    v2 = jnp.full_like(v1, -jnp.inf)
    for j in range(EXPERTS_PER_GROUP):
        better = (i1 != j) & ((ms[j] > v2) | (i2 < 0))
        i2 = jnp.where(better, j, i2)
        v2 = jnp.where(better, ms[j], v2)
    p1 = mp[0]
    p2 = mp[0]
    for j in range(1, EXPERTS_PER_GROUP):
        p1 = jnp.where(i1 == j, mp[j], p1)
        p2 = jnp.where(i2 == j, mp[j], p2)
    den = p1 + p2
    w1 = p1 / den
    w2 = p2 / den
    first_low = i1 < i2
    lo = jnp.where(first_low, i1, i2)
    hi = jnp.where(first_low, i2, i1)
    pair = jnp.where(lo == 0, hi - 1, jnp.where(lo == 1, hi + 1, 5))
    cls = gbest * N_PAIRS + pair
    return cls, jnp.where(first_low, w1, w2), jnp.where(first_low, w2, w1)


def _mix_kernel(x_ref, hf_ref, hb_ref, ug_ref, of_ref, ob_ref, uhg_ref, ng_ref, wo_ref, g1_ref, b1_ref,
                wr_ref, br_ref, x1w_ref, cls_ref, rank_ref, cnt_ref, carry_scr, *, alpha):
    tm = x_ref.shape[0]

    @pl.when(pl.program_id(0) == 0)
    def _():
        carry_scr[...] = jnp.zeros_like(carry_scr)

    lru = (hf_ref[...] + hb_ref[...]) * jax.nn.gelu(ug_ref[...], approximate=True)
    o = of_ref[...] + ob_ref[...]
    heads = []
    for h in range(HG_HEADS):
        oh = o[:, h * HG_DV:(h + 1) * HG_DV]
        heads.append(oh * lax.rsqrt(jnp.mean(oh * oh, axis=-1, keepdims=True) + RMS_EPS))
    hg = jnp.concatenate(heads, axis=1) * ng_ref[...] * _silu(uhg_ref[...])
    mix_in = jnp.concatenate([lru, hg], axis=1).astype(BF16)
    mix = jnp.dot(mix_in, wo_ref[...], preferred_element_type=F32)
    x1 = _layer_norm(alpha * x_ref[...] + mix, g1_ref[...], b1_ref[...])
    def split(v):
        hi = v.astype(BF16)
        return hi, (v - hi.astype(F32)).astype(BF16)

    def dot_nt(a, b):
        return lax.dot_general(a, b, (((1,), (1,)), ((), ())), preferred_element_type=F32)

    wr_hi, wr_lo = split(wr_ref[...])
    x_hi, x_lo = split(x1)
    logits_t = dot_nt(wr_hi, x_hi) + (dot_nt(wr_hi, x_lo) + dot_nt(wr_lo, x_hi))
    cls, w_lo, w_hi = _route(logits_t, br_ref[...])
    rid = lax.broadcasted_iota(jnp.int32, (LANES, tm), 0)
    route_t = jnp.where(rid == 0, w_lo, jnp.where(rid == 1, w_hi, 0.0))
    x1w_ref[...] = jnp.concatenate([x1, route_t.T], axis=1)
    cid = lax.broadcasted_iota(jnp.int32, (CLASS_ROWS, tm), 0)
    onehot = jnp.where(cid == cls, 1.0, 0.0)
    si = lax.broadcasted_iota(jnp.int32, (tm, tm), 0)
    ti = lax.broadcasted_iota(jnp.int32, (tm, tm), 1)
    upper = jnp.where(si <= ti, 1.0, 0.0).astype(BF16)
    prefix = jnp.dot(onehot.astype(BF16), upper, preferred_element_type=F32)
    carry = carry_scr[...]
    rank = jnp.sum(onehot * (prefix - 1.0 + carry), axis=0, keepdims=True)
    cls_ref[...] = cls
    rank_ref[...] = rank.astype(jnp.int32)
    carry = carry + prefix[:, tm - 1:tm]
    carry_scr[...] = carry
    cnt_ref[...] = jnp.broadcast_to(carry, cnt_ref.shape)


def _mix(x2d, hf, hb, u2d, of, ob, norm_g, w_out_bf, ln_g, ln_b, w_router_t, b_router, alpha):
    t, d = x2d.shape
    w = LRU_WIDTH
    tm = min(TOKEN_TILE, t)
    assert t % tm == 0
    nt = t // tm
    row = lambda i: (i, 0)
    const = lambda i: (0, 0)
    in_specs = [
        pl.BlockSpec((tm, d), row),
        pl.BlockSpec((tm, w), row),
        pl.BlockSpec((tm, w), row),
        pl.BlockSpec((tm, w), lambda i: (i, COL_LRU_GATE)),
        pl.BlockSpec((tm, w), row),
        pl.BlockSpec((tm, w), row),
        pl.BlockSpec((tm, w), lambda i: (i, COL_G)),
        pl.BlockSpec((1, w), const),
        pl.BlockSpec((2 * w, d), const),
        pl.BlockSpec((1, d), const),
        pl.BlockSpec((1, d), const),
        pl.BlockSpec((N_EXPERTS, d), const),
        pl.BlockSpec((N_EXPERTS, 1), const),
    ]
    tok_spec = pl.BlockSpec((None, 1, tm), lambda i: (i, 0, 0))
    out_specs = [pl.BlockSpec((tm, d + LANES), row), tok_spec, tok_spec, pl.BlockSpec((CLASS_ROWS, LANES), const)]
    x1w, cls, rank, cnt = pl.pallas_call(
        functools.partial(_mix_kernel, alpha=alpha),
        grid=(nt,), in_specs=in_specs, out_specs=out_specs,
        out_shape=[jax.ShapeDtypeStruct((t, d + LANES), F32), jax.ShapeDtypeStruct((nt, 1, tm), jnp.int32),
                   jax.ShapeDtypeStruct((nt, 1, tm), jnp.int32), jax.ShapeDtypeStruct((CLASS_ROWS, LANES), F32)],
        scratch_shapes=[pltpu.VMEM((CLASS_ROWS, 1), F32)],
        compiler_params=_params(1), name="mix_ln_route",
    )(x2d, hf, hb, u2d, of, ob, u2d, norm_g.reshape(1, w), w_out_bf, ln_g.reshape(1, d), ln_b.reshape(1, d),
      w_router_t, b_router.reshape(N_EXPERTS, 1))
    return x1w, cls.reshape(t), rank.reshape(t), cnt[:N_CLASSES, 0].astype(jnp.int32)


def _route_plan(cls, rank, counts, t, tm):
    nt = t // tm + N_CLASSES
    tiles_c = (counts + tm - 1) // tm
    tile_end = jnp.cumsum(tiles_c)
    pos = ((tile_end - tiles_c) * tm)[cls] + rank
    last_tile = jnp.where(tiles_c > 0, tile_end - 1, -1).astype(jnp.int32)
    used = tile_end[-1:].astype(jnp.int32)
    xt = jnp.minimum(jnp.arange(nt, dtype=jnp.int32), tile_end[-1] - 1)
    tile_cls = jnp.sum((tile_end[None, :] <= xt[:, None]).astype(jnp.int32), axis=1)
    group = tile_cls // N_PAIRS
    pair = tile_cls % N_PAIRS
    ea = group * EXPERTS_PER_GROUP + jnp.array(PAIR_LO, jnp.int32)[pair]
    eb = group * EXPERTS_PER_GROUP + jnp.array(PAIR_HI, jnp.int32)[pair]
    return pos, last_tile, used, xt, ea, eb


def _dispatch_kernel(last_ref, used_ref, pos_ref, x_ref, xs_hbm, zbuf, xbuf, sem_z, sem, *, td, tm, n_tiles):
    i = pl.program_id(0)
    slot = lax.rem(i, 2)

    def zero_tile(j):
        cp = pltpu.make_async_copy(zbuf, xs_hbm.at[pl.ds(pl.multiple_of(j * tm, tm), tm)], sem_z.at[0])
        cp.start()
        cp.wait()

    @pl.when(i == 0)
    def _():
        zbuf[...] = jnp.zeros_like(zbuf)
        for c in range(N_CLASSES):
            @pl.when(last_ref[c] >= 0)
            def _():
                zero_tile(last_ref[c])

        def tail(j, carry):
            zero_tile(j)
            return carry
        lax.fori_loop(used_ref[0], n_tiles, tail, 0)

    def wait_slot(s):
        pltpu.make_async_copy(xbuf.at[s], xs_hbm.at[pl.ds(0, td)], sem.at[s]).wait()

    @pl.when(i >= 2)
    def _():
        wait_slot(slot)

    xbuf[slot] = x_ref[...]
    for r in range(td):
        pltpu.make_async_copy(xbuf.at[slot, pl.ds(r, 1)], xs_hbm.at[pl.ds(pos_ref[0, r], 1)], sem.at[slot]).start()

    @pl.when(i == pl.num_programs(0) - 1)
    def _():
        wait_slot(slot)

        @pl.when(i >= 1)
        def _():
            wait_slot(1 - slot)


def _dispatch(x1w, pos, last_tile, used, tm, n_tiles):
    t, w = x1w.shape
    nt, _, td = pos.shape
    grid_spec = pltpu.PrefetchScalarGridSpec(
        num_scalar_prefetch=2, grid=(nt,),
        in_specs=[pl.BlockSpec((None, 1, td), lambda i, lt, us: (i, 0, 0), memory_space=pltpu.SMEM),
                  pl.BlockSpec((td, w), lambda i, lt, us: (i, 0))],
        out_specs=pl.BlockSpec(memory_space=pl.ANY),
        scratch_shapes=[pltpu.VMEM((tm, w), F32), pltpu.VMEM((2, td, w), F32),
                        pltpu.SemaphoreType.DMA((1,)), pltpu.SemaphoreType.DMA((2,))])
    return pl.pallas_call(
        functools.partial(_dispatch_kernel, td=td, tm=tm, n_tiles=n_tiles), grid_spec=grid_spec,
        out_shape=jax.ShapeDtypeStruct((n_tiles * tm, w), F32),
        compiler_params=_params(1), name="dispatch",
    )(last_tile, used, pos, x1w)


def _start_row_gather(idx_ref, src_hbm, buf, sem, slot, rows):
    for r in range(rows):
        pltpu.make_async_copy(src_hbm.at[pl.ds(idx_ref[0, r], 1)], buf.at[slot, pl.ds(r, 1)], sem.at[slot]).start()


def _wait_row_gather(src_hbm, buf, sem, slot, rows):
    pltpu.make_async_copy(src_hbm.at[pl.ds(0, rows)], buf.at[slot], sem.at[slot]).wait()


def _moe_kernel(xt_ref, ea_ref, eb_ref, x_ref, wga_ref, wua_ref, wda_ref, wgb_ref, wub_ref, wdb_ref, y_ref, *, d):
    del xt_ref, ea_ref, eb_ref
    xw = x_ref[...]
    x = xw[:, :d].astype(BF16)
    w_lo = xw[:, d:d + 1]
    w_hi = xw[:, d + 1:d + 2]

    def expert(wg_ref, wu_ref, wd_ref):
        hid = _silu(jnp.dot(x, wg_ref[...], preferred_element_type=F32)) * jnp.dot(
            x, wu_ref[...], preferred_element_type=F32)
        return jnp.dot(hid.astype(BF16), wd_ref[...], preferred_element_type=F32)

    y_ref[...] = w_lo * expert(wga_ref, wua_ref, wda_ref) + w_hi * expert(wgb_ref, wub_ref, wdb_ref)


def _moe(xs, xt, ea, eb, tm, wg_bf, wu_bf, wd_bf):
    d = xs.shape[1] - LANES
    _, _, dff = wg_bf.shape
    nt = xs.shape[0] // tm
    up_a = pl.BlockSpec((None, d, dff), lambda i, xt, ea, eb: (ea[i], 0, 0))
    down_a = pl.BlockSpec((None, dff, d), lambda i, xt, ea, eb: (ea[i], 0, 0))
    up_b = pl.BlockSpec((None, d, dff), lambda i, xt, ea, eb: (eb[i], 0, 0))
    down_b = pl.BlockSpec((None, dff, d), lambda i, xt, ea, eb: (eb[i], 0, 0))
    grid_spec = pltpu.PrefetchScalarGridSpec(
        num_scalar_prefetch=3, grid=(nt,),
        in_specs=[pl.BlockSpec((tm, d + LANES), lambda i, xt, ea, eb: (xt[i], 0)),
                  up_a, up_a, down_a, up_b, up_b, down_b],
        out_specs=pl.BlockSpec((tm, d), lambda i, xt, ea, eb: (i, 0)))
    return pl.pallas_call(
        functools.partial(_moe_kernel, d=d), grid_spec=grid_spec,
        out_shape=jax.ShapeDtypeStruct((nt * tm, d), F32),
        compiler_params=_params(1), name="moe_routed",
    )(xt, ea, eb, xs, wg_bf, wu_bf, wd_bf, wg_bf, wu_bf, wd_bf)


def _ln2_gathered(pos_ref, posn_ref, x1_ref, ys_hbm, g_ref, b_ref, ybuf, sem, alpha):
    i = pl.program_id(0)
    tm = x1_ref.shape[0]
    slot = lax.rem(i, 2)

    @pl.when(i == 0)
    def _():
        _start_row_gather(pos_ref, ys_hbm, ybuf, sem, 0, tm)

    _wait_row_gather(ys_hbm, ybuf, sem, slot, tm)
    _start_row_gather(posn_ref, ys_hbm, ybuf, sem, 1 - slot, tm)
    return _layer_norm(alpha * x1_ref[...] + ybuf[slot], g_ref[...], b_ref[...])


def _ln2_finish(ys_hbm, ybuf, sem):
    i = pl.program_id(0)

    @pl.when(i == pl.num_programs(0) - 1)
    def _():
        _wait_row_gather(ys_hbm, ybuf, sem, 1 - lax.rem(i, 2), ybuf.shape[1])


def _ln2_in_proj_kernel(pos_ref, posn_ref, x1_ref, ys_hbm, g_ref, b_ref, w_ref, u_ref, x2_ref, ybuf, sem, *, alpha):
    x2 = _ln2_gathered(pos_ref, posn_ref, x1_ref, ys_hbm, g_ref, b_ref, ybuf, sem, alpha)
    x2_ref[...] = x2
    u_ref[...] = jnp.dot(x2.astype(BF16), w_ref[...], preferred_element_type=F32)
    _ln2_finish(ys_hbm, ybuf, sem)


def _ln2_kernel(pos_ref, posn_ref, x1_ref, ys_hbm, g_ref, b_ref, x2_ref, ybuf, sem, *, alpha):
    x2_ref[...] = _ln2_gathered(pos_ref, posn_ref, x1_ref, ys_hbm, g_ref, b_ref, ybuf, sem, alpha)
    _ln2_finish(ys_hbm, ybuf, sem)


def _ln2(x1w, ys, pos, ln_g, ln_b, alpha, w_in_bf=None):
    t = x1w.shape[0]
    d = ys.shape[1]
    nt, _, tm = pos.shape
    row = lambda i: (i, 0)
    const = lambda i: (0, 0)
    x_spec = pl.BlockSpec((tm, d), row)
    vec_spec = pl.BlockSpec((1, d), const)
    in_specs = [pl.BlockSpec((None, 1, tm), lambda i: (i, 0, 0), memory_space=pltpu.SMEM),
                pl.BlockSpec((None, 1, tm), lambda i: (jnp.minimum(i + 1, nt - 1), 0, 0), memory_space=pltpu.SMEM),
                x_spec, pl.BlockSpec(memory_space=pl.ANY), vec_spec, vec_spec]
    scratch = [pltpu.VMEM((2, tm, d), F32), pltpu.SemaphoreType.DMA((2,))]
    args = (pos, pos, x1w, ys, ln_g.reshape(1, d), ln_b.reshape(1, d))
    if w_in_bf is None:
        return pl.pallas_call(
            functools.partial(_ln2_kernel, alpha=alpha), grid=(nt,), in_specs=in_specs, out_specs=x_spec,
            out_shape=jax.ShapeDtypeStruct((t, d), F32), scratch_shapes=scratch,
            compiler_params=_params(1), name="ln2")(*args)
    cols = w_in_bf.shape[1]
    return pl.pallas_call(
        functools.partial(_ln2_in_proj_kernel, alpha=alpha), grid=(nt,),
        in_specs=in_specs + [pl.BlockSpec((d, cols), const)],
        out_specs=[pl.BlockSpec((tm, cols), row), x_spec],
        out_shape=[jax.ShapeDtypeStruct((t, cols), F32), jax.ShapeDtypeStruct((t, d), F32)],
        scratch_shapes=scratch, compiler_params=_params(1), name="ln2_in_proj")(*args, w_in_bf)


def _block_diag(blocks):
    nb, bw, _ = blocks.shape
    eye = jnp.eye(nb, dtype=blocks.dtype)
    return jnp.einsum("hij,hg->higj", blocks, eye).reshape(nb * bw, nb * bw)


def _trunk(x, p):
    bsz, n, d = x.shape
    depth = p["w_in"].shape[0]
    alpha = (2 * depth) ** 0.25
    t = bsz * n
    u2d, x2d = _in_proj(x.reshape(t, d), p["w_in_bf"][0], (p["ln_in_g"], p["ln_in_b"]))
    for l in range(depth):
        u3 = u2d.reshape(bsz, n, -1)
        hs, os_ = [], []
        for di, rev in enumerate((False, True)):
            hs.append(_lru_scan(u3, p["conv_w"][l], p["conv_b"][l], p["w_gates_bf"][l][di], p["b_gates"][l][di],
                                p["lru_lambda"][l][di], rev))
            os_.append(_hgrn_scan(u3, p["hgrn_lb"][di], l, rev))
        x1w, cls, rank, counts = _mix(
            x2d, hs[0].reshape(t, -1), hs[1].reshape(t, -1), u2d, os_[0].reshape(t, -1), os_[1].reshape(t, -1),
            p["hgrn_norm_g"][l], p["w_out_bf"][l], p["ln1_g"][l], p["ln1_b"][l], p["w_router_t"], p["b_router"], alpha)
        tile_rows = min(MOE_ROW_TILE, t)
        pos, last_tile, used, xt, ea, eb = _route_plan(cls, rank, counts, t, tile_rows)
        tm = min(TOKEN_TILE, t)
        pos = pos.reshape(t // tm, 1, tm)
        xs = _dispatch(x1w, pos, last_tile, used, tile_rows, xt.shape[0])
        ys = _moe(xs, xt, ea, eb, tile_rows, p["w_gate_bf"][l], p["w_up_bf"][l], p["w_down_bf"][l])
        if l + 1 < depth:
            u2d, x2d = _ln2(x1w, ys, pos, p["ln2_g"][l], p["ln2_b"][l], alpha, p["w_in_bf"][l + 1])
        else:
            x2d = _ln2(x1w, ys, pos, p["ln2_g"][l], p["ln2_b"][l], alpha)
    return x2d.reshape(bsz, n, d)


def kernel(x_prompt, x_sample, ln_in_g, ln_in_b, w_in, conv_w, conv_b, lru_wa, lru_ba, lru_wx, lru_bx, lru_lambda,
           hgrn_lb, hgrn_norm_g, w_out, ln1_g, ln1_b, w_router, b_router, w_gate, w_up, w_down, ln2_g, ln2_b):
    depth = w_in.shape[0]
    d = w_in.shape[1]
    w_gates = jnp.stack([
        jnp.stack([jnp.concatenate([_block_diag(lru_wa[l, di]), _block_diag(lru_wx[l, di])], axis=1)
                   for di in range(2)]) for l in range(depth)])
    b_gates = jnp.concatenate([lru_ba, lru_bx], axis=-1)
    w_router_t = w_router.T
    p = dict(
        ln_in_g=ln_in_g, ln_in_b=ln_in_b, w_in_bf=w_in.astype(BF16), conv_w=conv_w, conv_b=conv_b,
        w_gates_bf=w_gates.astype(BF16), b_gates=b_gates, lru_lambda=lru_lambda, hgrn_lb=hgrn_lb,
        hgrn_norm_g=hgrn_norm_g, w_out_bf=w_out.astype(BF16), ln1_g=ln1_g, ln1_b=ln1_b,
        w_router_t=w_router_t, b_router=b_router, w_gate_bf=w_gate.astype(BF16), w_up_bf=w_up.astype(BF16),
        w_down_bf=w_down.astype(BF16), ln2_g=ln2_g, ln2_b=ln2_b, w_in=w_in)
    return (_trunk(x_prompt, p), _trunk(x_sample, p))
```
